```python
import jax, jax.numpy as jnp
from jax import lax
import numpy as np

D_MODEL = 2048
BATCH = 1
SEQ = 16384
DEPTH = 1

CHUNK = 64
EPS = 1e-6
MIX_WIDTH = D_MODEL

SSD_HEAD_DIM = 64
SSD_INNER = MIX_WIDTH // 2
SSD_HEADS = SSD_INNER // SSD_HEAD_DIM
SSD_GROUPS = 2
SSD_STATE = 128
SSD_CONV = 4
SSD_CONV_DIM = SSD_INNER + 2 * SSD_GROUPS * SSD_STATE

FOX_HEAD_DIM = 64
FOX_INNER = MIX_WIDTH - SSD_INNER
FOX_HEADS = FOX_INNER // FOX_HEAD_DIM
Q_BLOCK = 128

PEER_HEADS = 8
PEER_N_KEYS = 128
PEER_N_EXPERTS = PEER_N_KEYS * PEER_N_KEYS
PEER_KEY_DIM = 256
PEER_TOPK = 16
PEER_TOKEN_BLOCK = 128

IN_SPLITS = (SSD_INNER, SSD_CONV_DIM, SSD_HEADS, FOX_INNER, FOX_INNER, FOX_INNER, FOX_HEADS)
IN_PROJ_DIM = sum(IN_SPLITS)

kernel_name = "hymba_ssd_fox_peer_block"


def rmsnorm(x, w):
    xf = x.astype(jnp.float32)
    xf = xf * lax.rsqrt(jnp.mean(xf * xf, axis=-1, keepdims=True) + EPS)
    return (xf * w.astype(jnp.float32)).astype(x.dtype)


def ssd_chunked(xs, dt, A, Bm, Cm):
    b, l = xs.shape[:2]
    nc = l // CHUNK
    R = SSD_HEADS // SSD_GROUPS
    x = xs.astype(jnp.float32).reshape(b, nc, CHUNK, SSD_GROUPS, R, SSD_HEAD_DIM)
    dtc = dt.reshape(b, nc, CHUNK, SSD_GROUPS, R)
    Bc = Bm.astype(jnp.float32).reshape(b, nc, CHUNK, SSD_GROUPS, SSD_STATE)
    Cc = Cm.astype(jnp.float32).reshape(b, nc, CHUNK, SSD_GROUPS, SSD_STATE)
    a = dtc * A.reshape(SSD_GROUPS, R)
    a_cum = jnp.cumsum(a, axis=2)
    xdt = x * dtc[..., None]
    seg = a_cum[:, :, :, None] - a_cum[:, :, None]
    tri = jnp.tril(jnp.ones((CHUNK, CHUNK), dtype=bool))[None, None, :, :, None, None]
    L = jnp.exp(jnp.where(tri, seg, -jnp.inf))
    cb = jnp.einsum('bctgn,bcsgn->bctsg', Cc, Bc)
    y_diag = jnp.einsum('bctsg,bctsgr,bcsgrp->bctgrp', cb, L, xdt)
    decay_states = jnp.exp(a_cum[:, :, -1:] - a_cum)
    states = jnp.einsum('bcsgn,bcsgr,bcsgrp->bcgrpn', Bc, decay_states, xdt)
    chunk_decay = jnp.exp(a_cum[:, :, -1])

    def step(h, inp):
        s_c, d_c = inp
        return h * d_c[..., None, None] + s_c, h

    h0 = jnp.zeros((b, SSD_GROUPS, R, SSD_HEAD_DIM, SSD_STATE), jnp.float32)
    _, prev = lax.scan(step, h0, (jnp.moveaxis(states, 1, 0), jnp.moveaxis(chunk_decay, 1, 0)))
    prev = jnp.moveaxis(prev, 0, 1)
    y_off = jnp.einsum('bctgn,bcgrpn,bctgr->bctgrp', Cc, prev, jnp.exp(a_cum))
    return (y_diag + y_off).reshape(b, l, SSD_HEADS, SSD_HEAD_DIM)


def ssd_group(z, xbc, dt_raw, conv_w, conv_b, dt_bias, a_log, d_skip, ssd_norm_w):
    b, l = z.shape[:2]
    xbc = lax.conv_general_dilated(
        xbc, conv_w[:, None, :], window_strides=(1,), padding=[(SSD_CONV - 1, 0)],
        dimension_numbers=('NWC', 'WIO', 'NWC'), feature_group_count=SSD_CONV_DIM) + conv_b
    xbc = jax.nn.silu(xbc)
    xs, Bm, Cm = jnp.split(xbc, [SSD_INNER, SSD_INNER + SSD_GROUPS * SSD_STATE], axis=-1)
    xs = xs.reshape(b, l, SSD_HEADS, SSD_HEAD_DIM)
    Bm = Bm.reshape(b, l, SSD_GROUPS, SSD_STATE)
    Cm = Cm.reshape(b, l, SSD_GROUPS, SSD_STATE)
    dt = jax.nn.softplus(dt_raw.astype(jnp.float32) + dt_bias.astype(jnp.float32))
    A = -jnp.exp(a_log.astype(jnp.float32))
    y = ssd_chunked(xs, dt, A, Bm, Cm)
    y = y + d_skip.astype(jnp.float32)[:, None] * xs.astype(jnp.float32)
    y = y.reshape(b, l, SSD_INNER) * jax.nn.silu(z.astype(jnp.float32))
    y = y.reshape(b, l, SSD_GROUPS, SSD_INNER // SSD_GROUPS)
    y = y * lax.rsqrt(jnp.mean(y * y, axis=-1, keepdims=True) + EPS)
    y = y.reshape(b, l, SSD_INNER) * ssd_norm_w.astype(jnp.float32)
    return y.astype(z.dtype)


def fox_group(q, k, v, f_raw, fox_f_bias):
    b, l = q.shape[:2]
    q = q.reshape(b, l, FOX_HEADS, FOX_HEAD_DIM) * (FOX_HEAD_DIM ** -0.5)
    k = k.reshape(b, l, FOX_HEADS, FOX_HEAD_DIM)
    v = v.reshape(b, l, FOX_HEADS, FOX_HEAD_DIM)
    log_f = jax.nn.log_sigmoid(f_raw.astype(jnp.float32) + fox_f_bias.astype(jnp.float32))
    cum = jnp.transpose(jnp.cumsum(log_f, axis=1), (0, 2, 1))
    outs = []
    for i in range(l // Q_BLOCK):
        qs, qe = i * Q_BLOCK, (i + 1) * Q_BLOCK
        s = jnp.einsum('bqhd,bkhd->bhqk', q[:, qs:qe], k[:, :qe]).astype(jnp.float32)
        s = s + cum[:, :, qs:qe, None] - cum[:, :, None, :qe]
        mask = (qs + jnp.arange(Q_BLOCK))[:, None] >= jnp.arange(qe)[None, :]
        p = jax.nn.softmax(jnp.where(mask, s, -jnp.inf), axis=-1)
        outs.append(jnp.einsum('bhqk,bkhd->bqhd', p.astype(v.dtype), v[:, :qe]))
    return jnp.concatenate(outs, axis=1).reshape(b, l, FOX_INNER)


def peer(h, wq, k1, k2, u, v):
    b, l, d = h.shape
    t = h.reshape(-1, d)
    T = t.shape[0]
    q = (t @ wq).reshape(T, PEER_HEADS, 2, PEER_KEY_DIM // 2)
    s1 = jnp.einsum('thd,hnd->thn', q[:, :, 0], k1).astype(jnp.float32)
    s2 = jnp.einsum('thd,hnd->thn', q[:, :, 1], k2).astype(jnp.float32)
    v1, i1 = lax.top_k(s1, PEER_TOPK)
    v2, i2 = lax.top_k(s2, PEER_TOPK)
    cand = (v1[..., :, None] + v2[..., None, :]).reshape(T, PEER_HEADS, PEER_TOPK * PEER_TOPK)
    cand_idx = (i1[..., :, None] * PEER_N_KEYS + i2[..., None, :]).reshape(T, PEER_HEADS, PEER_TOPK * PEER_TOPK)
    sc, pos = lax.top_k(cand, PEER_TOPK)
    idx = jnp.take_along_axis(cand_idx, pos, axis=-1)
    g = jax.nn.softmax(sc, axis=-1).astype(t.dtype)

    def block(args):
        tb, ib, gb = args
        act = jax.nn.gelu(jnp.einsum('thkd,td->thk', u[ib], tb), approximate=False)
        return jnp.einsum('thk,thkd->td', gb * act, v[ib])

    nb = T // PEER_TOKEN_BLOCK
    out = lax.map(block, (t.reshape(nb, PEER_TOKEN_BLOCK, d),
                          idx.reshape(nb, PEER_TOKEN_BLOCK, PEER_HEADS, PEER_TOPK),
                          g.reshape(nb, PEER_TOKEN_BLOCK, PEER_HEADS, PEER_TOPK)))
    return out.reshape(b, l, d)


def setup_inputs(seed: int = 0) -> dict:
    key = jax.random.key(seed)
    ks = jax.random.split(key, 20)
    f32 = jnp.float32
    nrm = lambda k, shape: jax.random.normal(k, shape, f32)
    dt0 = jnp.exp(jax.random.uniform(ks[5], (SSD_HEADS,), f32, np.log(1e-3), np.log(1e-1)))
    return {
        "x": nrm(ks[0], (BATCH, SEQ, D_MODEL)),
        "ln1_w": 1.0 + 0.02 * nrm(ks[1], (D_MODEL,)),
        "w_in": nrm(ks[2], (D_MODEL, IN_PROJ_DIM)) * D_MODEL ** -0.5,
        "conv_w": nrm(ks[3], (SSD_CONV, SSD_CONV_DIM)) * SSD_CONV ** -0.5,
        "conv_b": 0.02 * nrm(ks[4], (SSD_CONV_DIM,)),
        "dt_bias": dt0 + jnp.log(-jnp.expm1(-dt0)),
        "a_log": jnp.log(jax.random.uniform(ks[6], (SSD_HEADS,), f32, 1.0, 16.0)),
        "d_skip": 1.0 + 0.1 * nrm(ks[7], (SSD_HEADS,)),
        "ssd_norm_w": 1.0 + 0.02 * nrm(ks[8], (SSD_INNER,)),
        "fox_f_bias": jnp.linspace(1.0, 6.0, FOX_HEADS, dtype=f32) + 0.1 * nrm(ks[9], (FOX_HEADS,)),
        "w_out": nrm(ks[10], (MIX_WIDTH, D_MODEL)) * MIX_WIDTH ** -0.5,
        "ln2_w": 1.0 + 0.02 * nrm(ks[11], (D_MODEL,)),
        "peer_wq": nrm(ks[12], (D_MODEL, PEER_HEADS * PEER_KEY_DIM)) * D_MODEL ** -0.5,
        "peer_k1": nrm(ks[13], (PEER_HEADS, PEER_N_KEYS, PEER_KEY_DIM // 2)) * (PEER_KEY_DIM // 2) ** -0.5,
        "peer_k2": nrm(ks[14], (PEER_HEADS, PEER_N_KEYS, PEER_KEY_DIM // 2)) * (PEER_KEY_DIM // 2) ** -0.5,
        "peer_u": nrm(ks[15], (PEER_N_EXPERTS, D_MODEL)) * D_MODEL ** -0.5,
        "peer_v": nrm(ks[16], (PEER_N_EXPERTS, D_MODEL)) * PEER_HEADS ** -0.5,
        "lnf_w": 1.0 + 0.02 * nrm(ks[17], (D_MODEL,)),
    }


def reference(x, ln1_w, w_in, conv_w, conv_b, dt_bias, a_log, d_skip, ssd_norm_w,
              fox_f_bias, w_out, ln2_w, peer_wq, peer_k1, peer_k2, peer_u, peer_v, lnf_w):
    split_at = [int(s) for s in np.cumsum(IN_SPLITS)[:-1]]
    for _ in range(DEPTH):
        h = rmsnorm(x, ln1_w)
        proj = h @ w_in
        z, xbc, dt_raw, q, k, v, f_raw = jnp.split(proj, split_at, axis=-1)
        y_ssd = ssd_group(z, xbc, dt_raw, conv_w, conv_b, dt_bias, a_log, d_skip, ssd_norm_w)
        y_fox = fox_group(q, k, v, f_raw, fox_f_bias)
        x = x + jnp.concatenate([y_ssd, y_fox], axis=-1) @ w_out
        x = x + peer(rmsnorm(x, ln2_w), peer_wq, peer_k1, peer_k2, peer_u, peer_v)
    return rmsnorm(x, lnf_w)
```

```python
import functools

import jax
import jax.numpy as jnp
from jax import lax
from jax.experimental import pallas as pl
from jax.experimental.pallas import tpu as pltpu

F32 = jnp.float32
BF16 = jnp.bfloat16

D_MODEL = 2048
EPS = 1e-6

SSD_HEAD_DIM = 64
SSD_INNER = 1024
SSD_HEADS = 16
SSD_GROUPS = 2
SSD_STATE = 128
SSD_CONV = 4
SSD_BC = 2 * SSD_GROUPS * SSD_STATE
SSD_GROUP_WIDTH = SSD_INNER // SSD_GROUPS

FOX_HEAD_DIM = 64
FOX_INNER = 1024
FOX_HEADS = 16

PEER_HEADS = 8
PEER_N_KEYS = 128
PEER_N_EXPERTS = PEER_N_KEYS * PEER_N_KEYS
PEER_KEY_DIM = 256
PEER_TOPK = 16

COL_Z = 0
COL_XS = 1024
COL_BC = 2048
COL_Q = 2560
COL_K = 3584
COL_V = 4608
MAIN_DIM = 5632
SMALL_DIM = 128

LANES = 128
VMEM_LIMIT = 56 * 1024 * 1024

NEG_BIG = -1e30

_NT = (((1,), (1,)), ((), ()))
_TN = (((0,), (0,)), ((), ()))


def _softplus(x):
    return jnp.maximum(x, 0.0) + jnp.log1p(jnp.exp(-jnp.abs(x)))


def _silu(x):
    return x * jax.nn.sigmoid(x)


def _split3(x):
    hi = x.astype(BF16)
    r = x - hi.astype(F32)
    mid = r.astype(BF16)
    lo = (r - mid.astype(F32)).astype(BF16)
    return hi, mid, lo


def _dot3_lhs(x, w):
    a, b, c = _split3(x)
    d = lambda p: jnp.dot(p, w, preferred_element_type=F32)
    return d(a) + d(b) + d(c)


def _dot3_rhs(w, x):
    a, b, c = _split3(x)
    d = lambda p: jnp.dot(w, p, preferred_element_type=F32)
    return d(a) + d(b) + d(c)


def _inproj_kernel(x_ref, lnw_ref, wm_ref, ws_ref, main_ref, small_ref, h_ref):
    @pl.when(pl.program_id(1) == 0)
    def _():
        x = x_ref[...]
        h = x * lax.rsqrt(jnp.mean(x * x, axis=-1, keepdims=True) + EPS) * lnw_ref[...]
        hb = h.astype(BF16)
        h_ref[...] = hb
        small_ref[...] = jnp.dot(hb, ws_ref[...], preferred_element_type=F32)

    main_ref[...] = jnp.dot(h_ref[...], wm_ref[...], preferred_element_type=F32).astype(BF16)


def _inproj(x2d, ln1_w, w_main, w_small, *, tm, tn):
    L = x2d.shape[0]
    return pl.pallas_call(
        _inproj_kernel,
        grid=(L // tm, MAIN_DIM // tn),
        in_specs=[
            pl.BlockSpec((tm, D_MODEL), lambda i, j: (i, 0)),
            pl.BlockSpec((1, D_MODEL), lambda i, j: (0, 0)),
            pl.BlockSpec((D_MODEL, tn), lambda i, j: (0, j)),
            pl.BlockSpec((D_MODEL, SMALL_DIM), lambda i, j: (0, 0)),
        ],
        out_specs=[
            pl.BlockSpec((tm, tn), lambda i, j: (i, j)),
            pl.BlockSpec((tm, SMALL_DIM), lambda i, j: (i, 0)),
        ],
        out_shape=[
            jax.ShapeDtypeStruct((L, MAIN_DIM), BF16),
            jax.ShapeDtypeStruct((L, SMALL_DIM), F32),
        ],
        scratch_shapes=[pltpu.VMEM((tm, D_MODEL), BF16)],
        compiler_params=pltpu.CompilerParams(
            dimension_semantics=("parallel", "arbitrary"), vmem_limit_bytes=VMEM_LIMIT),
        name="inproj",
    )(x2d, ln1_w.reshape(1, D_MODEL), w_main, w_small)


def _ssd_kernel(z_ref, xs_ref, bc_ref, small_ref, smallT_ref,
                cwx_ref, cbx_ref, cwbc_ref, cbbc_ref,
                dtb_ref, dtbT_ref, alog_ref, alogT_ref, fb_ref,
                dfull_ref, normw_ref, expand_ref,
                y_ref, cum_ref,
                extx_ref, extbc_ref, state_ref, fcarry_ref, *, q):
    c = pl.program_id(0)
    tail = 8

    @pl.when(c == 0)
    def _():
        extx_ref[0:tail, :] = jnp.zeros((tail, SSD_INNER), F32)
        extbc_ref[0:tail, :] = jnp.zeros((tail, SSD_BC), F32)
        state_ref[...] = jnp.zeros(state_ref.shape, F32)
        fcarry_ref[...] = jnp.zeros(fcarry_ref.shape, F32)

    extx_ref[tail:tail + q, :] = xs_ref[...].astype(F32)
    extbc_ref[tail:tail + q, :] = bc_ref[...].astype(F32)

    def conv(ext_ref, w_ref, b_ref):
        acc = b_ref[...]
        for k in range(SSD_CONV):
            off = tail - (SSD_CONV - 1) + k
            acc = acc + ext_ref[off:off + q, :] * w_ref[k:k + 1, :]
        return acc

    xs = _silu(conv(extx_ref, cwx_ref, cbx_ref))
    bc = _silu(conv(extbc_ref, cwbc_ref, cbbc_ref))
    extx_ref[0:tail, :] = extx_ref[q:q + tail, :]
    extbc_ref[0:tail, :] = extbc_ref[q:q + tail, :]

    row = lax.broadcasted_iota(jnp.int32, (q, q), 0)
    col = lax.broadcasted_iota(jnp.int32, (q, q), 1)
    lower = row >= col
    tri = lower.astype(BF16)
    upper = (row <= col).astype(BF16)

    small = small_ref[...]
    smallT = smallT_ref[...]
    neg_a = -jnp.exp(alog_ref[...])
    neg_aT = -jnp.exp(alogT_ref[...])
    dt = _softplus(small[:, 0:SSD_HEADS] + dtb_ref[...])
    dtT = _softplus(smallT[0:SSD_HEADS, :] + dtbT_ref[...])
    a_cum = _dot3_rhs(tri, dt * neg_a)
    a_cumT = _dot3_lhs(dtT * neg_aT, upper)
    a_last = a_cum[q - 1:q, :]

    logf = -_softplus(-(small[:, SSD_HEADS:SSD_HEADS + FOX_HEADS] + fb_ref[...]))
    cumf = fcarry_ref[...] + _dot3_rhs(tri, logf)
    cum_ref[...] = cumf
    fcarry_ref[...] = cumf[q - 1:q, :]

    expand = expand_ref[...]
    dt_full = _dot3_lhs(dt, expand)
    ea_full = _dot3_lhs(jnp.exp(a_cum), expand)
    ds_full = _dot3_lhs(jnp.exp(a_last - a_cum), expand)

    xdt = xs * dt_full
    xdt_b = xdt.astype(BF16)
    xw_b = (xdt * ds_full).astype(BF16)
    bc_b = bc.astype(BF16)
    lane = lax.broadcasted_iota(jnp.int32, (q, LANES), 1)

    pieces = []
    for g in range(SSD_GROUPS):
        b_g = bc_b[:, g * SSD_STATE:(g + 1) * SSD_STATE]
        c_g = bc_b[:, (SSD_GROUPS + g) * SSD_STATE:(SSD_GROUPS + g + 1) * SSD_STATE]
        cb = lax.dot_general(c_g, b_g, _NT, preferred_element_type=F32)
        heads_per_group = SSD_HEADS // SSD_GROUPS
        for pp in range(heads_per_group // 2):
            pair = g * (heads_per_group // 2) + pp
            xp = xdt_b[:, pair * LANES:(pair + 1) * LANES]
            ys = []
            for hh in range(2):
                h = 2 * pair + hh
                seg = a_cum[:, h:h + 1] - a_cumT[h:h + 1, :]
                decay = jnp.exp(jnp.where(lower, seg, -jnp.inf))
                m = (cb * decay).astype(BF16)
                ys.append(jnp.dot(m, xp, preferred_element_type=F32))
            pieces.append(jnp.where(lane < SSD_HEAD_DIM, ys[0], ys[1]))
    y_diag = jnp.concatenate(pieces, axis=1)

    offs = []
    for g in range(SSD_GROUPS):
        b_g = bc_b[:, g * SSD_STATE:(g + 1) * SSD_STATE]
        c_g = bc_b[:, (SSD_GROUPS + g) * SSD_STATE:(SSD_GROUPS + g + 1) * SSD_STATE]
        gs = slice(g * SSD_GROUP_WIDTH, (g + 1) * SSD_GROUP_WIDTH)
        st = state_ref[g]
        offs.append(jnp.dot(c_g, st.astype(BF16), preferred_element_type=F32))
        state_ref[g] = st * ea_full[q - 1:q, gs] + lax.dot_general(
            b_g, xw_b[:, gs], _TN, preferred_element_type=F32)
    y_off = jnp.concatenate(offs, axis=1) * ea_full

    y = y_diag + y_off + dfull_ref[...] * xs
    y = y * _silu(z_ref[...].astype(F32))
    normed = []
    for g in range(SSD_GROUPS):
        yg = y[:, g * SSD_GROUP_WIDTH:(g + 1) * SSD_GROUP_WIDTH]
        normed.append(yg * lax.rsqrt(jnp.mean(yg * yg, axis=-1, keepdims=True) + EPS))
    y_ref[...] = (jnp.concatenate(normed, axis=1) * normw_ref[...]).astype(BF16)


def _ssd(main, small, smallT, conv_w, conv_b, dt_bias, a_log, d_skip, ssd_norm_w, fox_f_bias, *, q):
    L = main.shape[0]
    expand = (jnp.arange(SSD_INNER)[None, :] // SSD_HEAD_DIM == jnp.arange(SSD_HEADS)[:, None]).astype(BF16)
    const = lambda shape: pl.BlockSpec(shape, lambda c: (0,) * len(shape))
    return pl.pallas_call(
        functools.partial(_ssd_kernel, q=q),
        grid=(L // q,),
        in_specs=[
            pl.BlockSpec((q, SSD_INNER), lambda c: (c, COL_Z // SSD_INNER)),
            pl.BlockSpec((q, SSD_INNER), lambda c: (c, COL_XS // SSD_INNER)),
            pl.BlockSpec((q, SSD_BC), lambda c: (c, COL_BC // SSD_BC)),
            pl.BlockSpec((q, SMALL_DIM), lambda c: (c, 0)),
            pl.BlockSpec((SMALL_DIM, q), lambda c: (0, c)),
            const((SSD_CONV, SSD_INNER)), const((1, SSD_INNER)),
            const((SSD_CONV, SSD_BC)), const((1, SSD_BC)),
            const((1, SSD_HEADS)), const((SSD_HEADS, 1)),
            const((1, SSD_HEADS)), const((SSD_HEADS, 1)),
            const((1, FOX_HEADS)),
            const((1, SSD_INNER)), const((1, SSD_INNER)),
            const((SSD_HEADS, SSD_INNER)),
        ],
        out_specs=[
            pl.BlockSpec((q, SSD_INNER), lambda c: (c, 0)),
            pl.BlockSpec((q, FOX_HEADS), lambda c: (c, 0)),
        ],
        out_shape=[
            jax.ShapeDtypeStruct((L, SSD_INNER), BF16),
            jax.ShapeDtypeStruct((L, FOX_HEADS), F32),
        ],
        scratch_shapes=[
            pltpu.VMEM((q + 8, SSD_INNER), F32),
            pltpu.VMEM((q + 8, SSD_BC), F32),
            pltpu.VMEM((SSD_GROUPS, SSD_STATE, SSD_GROUP_WIDTH), F32),
            pltpu.VMEM((1, FOX_HEADS), F32),
        ],
        compiler_params=pltpu.CompilerParams(
            dimension_semantics=("arbitrary",), vmem_limit_bytes=VMEM_LIMIT),
        name="ssd",
    )(main, main, main, small, smallT,
      conv_w[:, :SSD_INNER], conv_b[:SSD_INNER].reshape(1, -1),
      conv_w[:, SSD_INNER:], conv_b[SSD_INNER:].reshape(1, -1),
      dt_bias.reshape(1, -1), dt_bias.reshape(-1, 1),
      a_log.reshape(1, -1), a_log.reshape(-1, 1),
      fox_f_bias.reshape(1, -1),
      jnp.repeat(d_skip, SSD_HEAD_DIM).reshape(1, -1), ssd_norm_w.reshape(1, -1),
      expand)


def _fox_kernel(cref_ref, q_ref, k_ref, v_ref, ck_ref, o_ref,
                qm_ref, m_ref, l_ref, acc_ref, *, tq, tk):
    p = pl.program_id(0)
    i = pl.program_id(1)
    j = pl.program_id(2)
    last = ((i + 1) * tq - 1) // tk
    lane = lax.broadcasted_iota(jnp.int32, (tq, LANES), 1)
    first_head = lane < FOX_HEAD_DIM

    @pl.when(j == 0)
    def _():
        qs = q_ref[...].astype(F32) * (FOX_HEAD_DIM ** -0.5)
        qm_ref[0] = jnp.where(first_head, qs, 0.0).astype(BF16)
        qm_ref[1] = jnp.where(first_head, 0.0, qs).astype(BF16)
        m_ref[...] = jnp.full(m_ref.shape, NEG_BIG, F32)
        l_ref[...] = jnp.zeros(l_ref.shape, F32)
        acc_ref[...] = jnp.zeros(acc_ref.shape, F32)

    @pl.when(j <= last)
    def _():
        k = k_ref[...]
        v = v_ref[...]
        rows = i * tq + lax.broadcasted_iota(jnp.int32, (tq, tk), 0)
        cols = j * tk + lax.broadcasted_iota(jnp.int32, (tq, tk), 1)
        causal = rows >= cols
        for hh in range(2):
            bias = cref_ref[i * FOX_HEADS + 2 * p + hh] - ck_ref[hh:hh + 1, :]
            s = lax.dot_general(qm_ref[hh], k, _NT, preferred_element_type=F32)
            s = jnp.where(causal, s + bias, NEG_BIG)
            m_prev = m_ref[hh][:, 0:1]
            m_new = jnp.maximum(m_prev, jnp.max(s, axis=1, keepdims=True))
            alpha = jnp.exp(m_prev - m_new)
            pr = jnp.exp(s - m_new)
            l_ref[hh] = alpha * l_ref[hh] + jnp.sum(pr, axis=1, keepdims=True)
            acc_ref[hh] = alpha * acc_ref[hh] + jnp.dot(pr.astype(BF16), v, preferred_element_type=F32)
            m_ref[hh] = jnp.broadcast_to(m_new, (tq, LANES))

    @pl.when(j == last)
    def _():
        o_ref[...] = jnp.where(first_head, acc_ref[0] / l_ref[0], acc_ref[1] / l_ref[1]).astype(BF16)


def _fox(main, cum, *, tq, tk):
    L = main.shape[0]
    nq = L // tq
    pairs = FOX_HEADS // 2
    cref = cum[::tq].reshape(-1)
    ckT = cum.T.reshape(pairs, 2, L)
    last = lambda i: ((i + 1) * tq - 1) // tk
    grid_spec = pltpu.PrefetchScalarGridSpec(
        num_scalar_prefetch=1,
        grid=(pairs, nq, L // tk),
        in_specs=[
            pl.BlockSpec((tq, LANES), lambda p, i, j, c: (i, COL_Q // LANES + p)),
            pl.BlockSpec((tk, LANES), lambda p, i, j, c: (jnp.minimum(j, last(i)), COL_K // LANES + p)),
            pl.BlockSpec((tk, LANES), lambda p, i, j, c: (jnp.minimum(j, last(i)), COL_V // LANES + p)),
            pl.BlockSpec((None, 2, tk), lambda p, i, j, c: (p, 0, jnp.minimum(j, last(i)))),
        ],
        out_specs=pl.BlockSpec((tq, LANES), lambda p, i, j, c: (i, p)),
        scratch_shapes=[
            pltpu.VMEM((2, tq, LANES), BF16),
            pltpu.VMEM((2, tq, LANES), F32),
            pltpu.VMEM((2, tq, LANES), F32),
            pltpu.VMEM((2, tq, LANES), F32),
        ],
    )
    return pl.pallas_call(
        functools.partial(_fox_kernel, tq=tq, tk=tk),
        grid_spec=grid_spec,
        out_shape=jax.ShapeDtypeStruct((L, FOX_INNER), BF16),
        compiler_params=pltpu.CompilerParams(
            dimension_semantics=("parallel", "parallel", "arbitrary"), vmem_limit_bytes=VMEM_LIMIT),
        name="fox",
    )(cref, main, main, main, ckT)


def _mid_kernel(x_ref, ys_ref, yf_ref, wo_ref, ln2_ref, wq_ref, k1_ref, k2_ref,
                x1_ref, h2T_ref, s1T_ref, s2T_ref):
    mix = jnp.dot(ys_ref[...], wo_ref[0:SSD_INNER, :], preferred_element_type=F32)
    mix = mix + jnp.dot(yf_ref[...], wo_ref[SSD_INNER:, :], preferred_element_type=F32)
    x1 = x_ref[...] + mix
    x1_ref[...] = x1
    h2 = x1 * lax.rsqrt(jnp.mean(x1 * x1, axis=-1, keepdims=True) + EPS) * ln2_ref[...]
    h2T_ref[...] = h2.T.astype(BF16)
    qp = jnp.dot(h2.astype(BF16), wq_ref[...], preferred_element_type=F32)
    half = PEER_KEY_DIM // 2
    for h in range(PEER_HEADS):
        q1 = qp[:, h * PEER_KEY_DIM:h * PEER_KEY_DIM + half].astype(BF16)
        q2 = qp[:, h * PEER_KEY_DIM + half:(h + 1) * PEER_KEY_DIM].astype(BF16)
        s1T_ref[h] = lax.dot_general(k1_ref[h], q1, _NT, preferred_element_type=F32)
        s2T_ref[h] = lax.dot_general(k2_ref[h], q2, _NT, preferred_element_type=F32)


def _mid(x2d, y_ssd, y_fox, w_out, ln2_w, wq, k1, k2, *, tm):
    L = x2d.shape[0]
    half = PEER_KEY_DIM // 2
    resident = lambda shape: pl.BlockSpec(shape, lambda i: (0,) * len(shape), pipeline_mode=pl.Buffered(1))
    return pl.pallas_call(
        _mid_kernel,
        grid=(L // tm,),
        in_specs=[
            pl.BlockSpec((tm, D_MODEL), lambda i: (i, 0)),
            pl.BlockSpec((tm, SSD_INNER), lambda i: (i, 0)),
            pl.BlockSpec((tm, FOX_INNER), lambda i: (i, 0)),
            resident((D_MODEL, D_MODEL)),
            resident((1, D_MODEL)),
            resident((D_MODEL, PEER_HEADS * PEER_KEY_DIM)),
            resident((PEER_HEADS, PEER_N_KEYS, half)),
            resident((PEER_HEADS, PEER_N_KEYS, half)),
        ],
        out_specs=[
            pl.BlockSpec((tm, D_MODEL), lambda i: (i, 0)),
            pl.BlockSpec((D_MODEL, tm), lambda i: (0, i)),
            pl.BlockSpec((PEER_HEADS, PEER_N_KEYS, tm), lambda i: (0, 0, i)),
            pl.BlockSpec((PEER_HEADS, PEER_N_KEYS, tm), lambda i: (0, 0, i)),
        ],
        out_shape=[
            jax.ShapeDtypeStruct((L, D_MODEL), F32),
            jax.ShapeDtypeStruct((D_MODEL, L), BF16),
            jax.ShapeDtypeStruct((PEER_HEADS, PEER_N_KEYS, L), F32),
            jax.ShapeDtypeStruct((PEER_HEADS, PEER_N_KEYS, L), F32),
        ],
        compiler_params=pltpu.CompilerParams(
            dimension_semantics=("parallel",), vmem_limit_bytes=VMEM_LIMIT),
        name="mid",
    )(x2d, y_ssd, y_fox, w_out, ln2_w.reshape(1, D_MODEL), wq, k1, k2)


def _top16(s, key_id):
    t = s.shape[1]
    slot = lax.broadcasted_iota(jnp.int32, (PEER_TOPK, t), 0)
    n_sentinel = jnp.int32(1 << 20)

    def body(r, carry):
        s, rank, vals = carry
        m = jnp.max(s, axis=0, keepdims=True)
        idx = jnp.min(jnp.where(s == m, key_id, n_sentinel), axis=0, keepdims=True)
        hit = key_id == idx
        rank = jnp.where(hit, r.astype(F32), rank)
        s = jnp.where(hit, -jnp.inf, s)
        vals = jnp.where(slot == r, m, vals)
        return s, rank, vals

    init = (s, jnp.full(s.shape, float(PEER_TOPK), F32), jnp.zeros((PEER_TOPK, t), F32))
    _, rank, vals = lax.fori_loop(0, PEER_TOPK, body, init)
    return rank, vals


def _topk_kernel(s1T_ref, s2T_ref, lim_ref, e1_ref, r2_ref, e2_ref):
    t = s1T_ref.shape[2]
    key_id = lax.broadcasted_iota(jnp.int32, (PEER_N_KEYS, t), 0)
    half_k = PEER_TOPK // 2
    n_cand = half_k * PEER_TOPK + half_k
    pos_main = lax.broadcasted_iota(jnp.int32, (half_k * PEER_TOPK, t), 0)
    pos_tail = (lax.broadcasted_iota(jnp.int32, (half_k, t), 0) + half_k) * PEER_TOPK
    pos = jnp.concatenate([pos_main, pos_tail], axis=0)
    slot = lax.broadcasted_iota(jnp.int32, (PEER_TOPK, t), 0)
    n_sentinel = jnp.int32(1 << 20)

    def head(h, _):
        s1 = s1T_ref[h]
        s2 = s2T_ref[h]
        rank1, v1 = _top16(s1, key_id)
        rank2, v2 = _top16(s2, key_id)

        cand = jnp.concatenate(
            [v1[a:a + 1, :] + v2 for a in range(half_k)] + [v1[half_k:, :] + v2[0:1, :]], axis=0)
        top = v1[0:1, :] + v2[0:1, :]

        def body(r, carry):
            cand, sel, z = carry
            m = jnp.max(cand, axis=0, keepdims=True)
            idx = jnp.min(jnp.where(cand == m, pos, n_sentinel), axis=0, keepdims=True)
            hit = pos == idx
            sel = jnp.where(hit, 1.0, sel)
            cand = jnp.where(hit, -jnp.inf, cand)
            return cand, sel, z + jnp.exp(m - top)

        _, sel, z = lax.fori_loop(
            0, PEER_TOPK, body, (cand, jnp.zeros((n_cand, t), F32), jnp.zeros((1, t), F32)))

        bcount = jnp.zeros((PEER_TOPK, t), F32)
        for a in range(half_k):
            cnt = jnp.sum(sel[a * PEER_TOPK:(a + 1) * PEER_TOPK, :], axis=0, keepdims=True)
            bcount = jnp.where(slot == a, cnt, bcount)
        tail = jnp.concatenate([jnp.zeros((half_k, t), F32), sel[half_k * PEER_TOPK:, :]], axis=0)
        bcount = jnp.where(slot >= half_k, tail, bcount)

        lim = jnp.zeros((PEER_N_KEYS, t), F32)
        for a in range(PEER_TOPK):
            lim = jnp.where(rank1 == float(a), bcount[a:a + 1, :], lim)

        lim_ref[h] = lim
        e1_ref[h] = jnp.exp(s1 - v1[0:1, :])
        r2_ref[h] = rank2
        e2_ref[h] = jnp.exp(s2 - v2[0:1, :]) / z
        return 0

    lax.fori_loop(0, PEER_HEADS, head, 0)


def _topk(s1T, s2T, *, tt):
    L = s1T.shape[2]
    spec = pl.BlockSpec((PEER_HEADS, PEER_N_KEYS, tt), lambda i: (0, 0, i))
    shape = jax.ShapeDtypeStruct((PEER_HEADS, PEER_N_KEYS, L), F32)
    return pl.pallas_call(
        _topk_kernel,
        grid=(L // tt,),
        in_specs=[spec, spec],
        out_specs=[spec] * 4,
        out_shape=[shape] * 4,
        compiler_params=pltpu.CompilerParams(
            dimension_semantics=("parallel",), vmem_limit_bytes=VMEM_LIMIT),
        name="topk",
    )(s1T, s2T)


def _peer_kernel(h2T_ref, u_ref, vT_ref, lim_ref, e1_ref, r2_ref, e2_ref, x1_ref, lnf_ref,
                 o_ref, acc_ref, wg_ref, *, et):
    e = pl.program_id(1)

    @pl.when(e == 0)
    def _():
        acc_ref[...] = jnp.zeros(acc_ref.shape, F32)

    act = jnp.dot(u_ref[...], h2T_ref[...], preferred_element_type=F32)
    for i in range(et // PEER_N_KEYS):
        i1 = e * (et // PEER_N_KEYS) + i
        w = None
        for h in range(PEER_HEADS):
            lim_row = lim_ref[h, pl.ds(i1, 1), :]
            e1_row = e1_ref[h, pl.ds(i1, 1), :]
            term = jnp.where(r2_ref[h] < lim_row, e2_ref[h] * e1_row, 0.0)
            w = term if w is None else w + term
        a = act[i * PEER_N_KEYS:(i + 1) * PEER_N_KEYS, :]
        gelu = 0.5 * a * (1.0 + lax.erf(a * (2.0 ** -0.5)))
        wg_ref[i * PEER_N_KEYS:(i + 1) * PEER_N_KEYS, :] = (w * gelu).astype(BF16)
    acc_ref[...] += jnp.dot(vT_ref[...], wg_ref[...], preferred_element_type=F32)

    @pl.when(e == pl.num_programs(1) - 1)
    def _():
        x2 = x1_ref[...] + acc_ref[...].T
        o_ref[...] = x2 * lax.rsqrt(jnp.mean(x2 * x2, axis=-1, keepdims=True) + EPS) * lnf_ref[...]


def _peer(h2T, u_b, vT_b, lim, e1, r2, e2, x1, lnf_w, *, tt, et):
    L = x1.shape[0]
    tok = pl.BlockSpec((PEER_HEADS, PEER_N_KEYS, tt), lambda t, e: (0, 0, t))
    return pl.pallas_call(
        functools.partial(_peer_kernel, et=et),
        grid=(L // tt, PEER_N_EXPERTS // et),
        in_specs=[
            pl.BlockSpec((D_MODEL, tt), lambda t, e: (0, t)),
            pl.BlockSpec((et, D_MODEL), lambda t, e: (e, 0)),
            pl.BlockSpec((D_MODEL, et), lambda t, e: (0, e)),
            tok, tok, tok, tok,
            pl.BlockSpec((tt, D_MODEL), lambda t, e: (t, 0)),
            pl.BlockSpec((1, D_MODEL), lambda t, e: (0, 0)),
        ],
        out_specs=pl.BlockSpec((tt, D_MODEL), lambda t, e: (t, 0)),
        out_shape=jax.ShapeDtypeStruct((L, D_MODEL), F32),
        scratch_shapes=[
            pltpu.VMEM((D_MODEL, tt), F32),
            pltpu.VMEM((et, tt), BF16),
        ],
        compiler_params=pltpu.CompilerParams(
            dimension_semantics=("parallel", "arbitrary"), vmem_limit_bytes=VMEM_LIMIT),
        name="peer",
    )(h2T, u_b, vT_b, lim, e1, r2, e2, x1, lnf_w.reshape(1, D_MODEL))


def _tiles(L):
    return dict(
        inproj_tm=min(512, L), inproj_tn=1408,
        ssd_q=min(128, L),
        fox_tq=min(512, L), fox_tk=min(512, L),
        mid_tm=min(256, L),
        topk_tt=min(128, L),
        peer_tt=min(512, L), peer_et=512,
    )


def kernel(x, ln1_w, w_in, conv_w, conv_b, dt_bias, a_log, d_skip, ssd_norm_w, fox_f_bias,
           w_out, ln2_w, peer_wq, peer_k1, peer_k2, peer_u, peer_v, lnf_w):
    B, L, D = x.shape
    assert B == 1 and D == D_MODEL
    t = _tiles(L)
    x2d = x.reshape(L, D)

    c_z, c_xbc, c_dt, c_q = 0, SSD_INNER, 2560, 2576
    c_f = c_q + 3 * FOX_INNER
    w_main = jnp.concatenate([w_in[:, c_z:c_dt], w_in[:, c_q:c_f]], axis=1).astype(BF16)
    w_small = jnp.concatenate(
        [w_in[:, c_dt:c_q], w_in[:, c_f:], jnp.zeros((D, SMALL_DIM - SSD_HEADS - FOX_HEADS), F32)],
        axis=1).astype(BF16)

    main, small = _inproj(x2d, ln1_w, w_main, w_small, tm=t["inproj_tm"], tn=t["inproj_tn"])
    y_ssd, cum = _ssd(main, small, small.T, conv_w, conv_b, dt_bias, a_log, d_skip, ssd_norm_w,
                      fox_f_bias, q=t["ssd_q"])
    y_fox = _fox(main, cum, tq=t["fox_tq"], tk=t["fox_tk"])
    x1, h2T, s1T, s2T = _mid(x2d, y_ssd, y_fox, w_out.astype(BF16), ln2_w, peer_wq.astype(BF16),
                             peer_k1.astype(BF16), peer_k2.astype(BF16), tm=t["mid_tm"])
    lim, e1, r2, e2 = _topk(s1T, s2T, tt=t["topk_tt"])
    out = _peer(h2T, peer_u.astype(BF16), peer_v.T.astype(BF16), lim, e1, r2, e2, x1, lnf_w,
                tt=t["peer_tt"], et=t["peer_et"])
    return out.reshape(B, L, D)
```

```python
import functools

import jax
import jax.numpy as jnp
from jax import lax
from jax.experimental import pallas as pl
from jax.experimental.pallas import tpu as pltpu

F32 = jnp.float32
BF16 = jnp.bfloat16

D_MODEL = 2048
EPS = 1e-6

SSD_HEAD_DIM = 64
SSD_INNER = 1024
SSD_HEADS = 16
SSD_GROUPS = 2
SSD_STATE = 128
SSD_CONV = 4
SSD_BC = 2 * SSD_GROUPS * SSD_STATE
SSD_GROUP_WIDTH = SSD_INNER // SSD_GROUPS

FOX_HEAD_DIM = 64
FOX_INNER = 1024
FOX_HEADS = 16

PEER_HEADS = 8
PEER_N_KEYS = 128
PEER_N_EXPERTS = PEER_N_KEYS * PEER_N_KEYS
PEER_KEY_DIM = 256
PEER_TOPK = 16

COL_Z = 0
COL_XS = 1024
COL_BC = 2048
COL_Q = 2560
COL_K = 3584
COL_V = 4608
MAIN_DIM = 5632
SMALL_DIM = 128

LANES = 128
VMEM_LIMIT = 56 * 1024 * 1024

NEG_BIG = -1e30
LOG2E = 1.4426950408889634

_NT = (((1,), (1,)), ((), ()))
_TN = (((0,), (0,)), ((), ()))


def _softplus(x):
    return jnp.maximum(x, 0.0) + jnp.log1p(jnp.exp(-jnp.abs(x)))


def _silu(x):
    return x * jax.nn.sigmoid(x)


def _split3(x):
    hi = x.astype(BF16)
    r = x - hi.astype(F32)
    mid = r.astype(BF16)
    lo = (r - mid.astype(F32)).astype(BF16)
    return hi, mid, lo


def _dot3_lhs(x, w):
    a, b, c = _split3(x)
    d = lambda p: jnp.dot(p, w, preferred_element_type=F32)
    return d(a) + d(b) + d(c)


def _dot3_rhs(w, x):
    a, b, c = _split3(x)
    d = lambda p: jnp.dot(w, p, preferred_element_type=F32)
    return d(a) + d(b) + d(c)


def _inproj_kernel(x_ref, lnw_ref, wm_ref, ws_ref, main_ref, small_ref, h_ref):
    @pl.when(pl.program_id(1) == 0)
    def _():
        x = x_ref[...]
        h = x * lax.rsqrt(jnp.mean(x * x, axis=-1, keepdims=True) + EPS) * lnw_ref[...]
        hb = h.astype(BF16)
        h_ref[...] = hb
        small_ref[...] = jnp.dot(hb, ws_ref[...], preferred_element_type=F32)

    main_ref[...] = jnp.dot(h_ref[...], wm_ref[...], preferred_element_type=F32).astype(BF16)


def _inproj(x2d, ln1_w, w_main, w_small, *, tm, tn):
    L = x2d.shape[0]
    return pl.pallas_call(
        _inproj_kernel,
        grid=(L // tm, MAIN_DIM // tn),
        in_specs=[
            pl.BlockSpec((tm, D_MODEL), lambda i, j: (i, 0)),
            pl.BlockSpec((1, D_MODEL), lambda i, j: (0, 0)),
            pl.BlockSpec((D_MODEL, tn), lambda i, j: (0, j)),
            pl.BlockSpec((D_MODEL, SMALL_DIM), lambda i, j: (0, 0)),
        ],
        out_specs=[
            pl.BlockSpec((tm, tn), lambda i, j: (i, j)),
            pl.BlockSpec((tm, SMALL_DIM), lambda i, j: (i, 0)),
        ],
        out_shape=[
            jax.ShapeDtypeStruct((L, MAIN_DIM), BF16),
            jax.ShapeDtypeStruct((L, SMALL_DIM), F32),
        ],
        scratch_shapes=[pltpu.VMEM((tm, D_MODEL), BF16)],
        compiler_params=pltpu.CompilerParams(
            dimension_semantics=("parallel", "arbitrary"), vmem_limit_bytes=VMEM_LIMIT),
        name="inproj",
    )(x2d, ln1_w.reshape(1, D_MODEL), w_main, w_small)


def _ssd_kernel(z_ref, xs_ref, bc_ref, small_ref, smallT_ref,
                cwx_ref, cbx_ref, cwbc_ref, cbbc_ref,
                dtb_ref, dtbT_ref, alog_ref, alogT_ref, fb_ref,
                dfull_ref, normw_ref, expand_ref,
                y_ref, cum_ref,
                extx_ref, extbc_ref, state_ref, fcarry_ref, *, q):
    c = pl.program_id(0)
    tail = 8

    @pl.when(c == 0)
    def _():
        extx_ref[0:tail, :] = jnp.zeros((tail, SSD_INNER), F32)
        extbc_ref[0:tail, :] = jnp.zeros((tail, SSD_BC), F32)
        state_ref[...] = jnp.zeros(state_ref.shape, F32)
        fcarry_ref[...] = jnp.zeros(fcarry_ref.shape, F32)

    extx_ref[tail:tail + q, :] = xs_ref[...].astype(F32)
    extbc_ref[tail:tail + q, :] = bc_ref[...].astype(F32)

    def conv(ext_ref, w_ref, b_ref):
        acc = b_ref[...]
        for k in range(SSD_CONV):
            off = tail - (SSD_CONV - 1) + k
            acc = acc + ext_ref[off:off + q, :] * w_ref[k:k + 1, :]
        return acc

    xs = _silu(conv(extx_ref, cwx_ref, cbx_ref))
    bc = _silu(conv(extbc_ref, cwbc_ref, cbbc_ref))
    extx_ref[0:tail, :] = extx_ref[q:q + tail, :]
    extbc_ref[0:tail, :] = extbc_ref[q:q + tail, :]

    row = lax.broadcasted_iota(jnp.int32, (q, q), 0)
    col = lax.broadcasted_iota(jnp.int32, (q, q), 1)
    lower = row >= col
    tri = lower.astype(BF16)
    upper = (row <= col).astype(BF16)

    small = small_ref[...]
    smallT = smallT_ref[...]
    neg_a = -jnp.exp(alog_ref[...])
    neg_aT = -jnp.exp(alogT_ref[...])
    dt = _softplus(small[:, 0:SSD_HEADS] + dtb_ref[...])
    dtT = _softplus(smallT[0:SSD_HEADS, :] + dtbT_ref[...])
    a_cum = _dot3_rhs(tri, dt * neg_a)
    a_cumT = _dot3_lhs(dtT * neg_aT, upper)
    a_last = a_cum[q - 1:q, :]

    logf = -_softplus(-(small[:, SSD_HEADS:SSD_HEADS + FOX_HEADS] + fb_ref[...]))
    cumf = fcarry_ref[...] + _dot3_rhs(tri, logf)
    cum_ref[...] = cumf
    fcarry_ref[...] = cumf[q - 1:q, :]

    expand = expand_ref[...]
    dt_full = _dot3_lhs(dt, expand)
    ea_full = _dot3_lhs(jnp.exp(a_cum), expand)
    ds_full = _dot3_lhs(jnp.exp(a_last - a_cum), expand)

    xdt = xs * dt_full
    xdt_b = xdt.astype(BF16)
    xw_b = (xdt * ds_full).astype(BF16)
    bc_b = bc.astype(BF16)
    lane = lax.broadcasted_iota(jnp.int32, (q, LANES), 1)

    pieces = []
    for g in range(SSD_GROUPS):
        b_g = bc_b[:, g * SSD_STATE:(g + 1) * SSD_STATE]
        c_g = bc_b[:, (SSD_GROUPS + g) * SSD_STATE:(SSD_GROUPS + g + 1) * SSD_STATE]
        cb = lax.dot_general(c_g, b_g, _NT, preferred_element_type=F32)
        heads_per_group = SSD_HEADS // SSD_GROUPS
        for pp in range(heads_per_group // 2):
            pair = g * (heads_per_group // 2) + pp
            xp = xdt_b[:, pair * LANES:(pair + 1) * LANES]
            ys = []
            for hh in range(2):
                h = 2 * pair + hh
                seg = a_cum[:, h:h + 1] - a_cumT[h:h + 1, :]
                decay = jnp.exp(jnp.where(lower, seg, -jnp.inf))
                m = (cb * decay).astype(BF16)
                ys.append(jnp.dot(m, xp, preferred_element_type=F32))
            pieces.append(jnp.where(lane < SSD_HEAD_DIM, ys[0], ys[1]))
    y_diag = jnp.concatenate(pieces, axis=1)

    offs = []
    for g in range(SSD_GROUPS):
        b_g = bc_b[:, g * SSD_STATE:(g + 1) * SSD_STATE]
        c_g = bc_b[:, (SSD_GROUPS + g) * SSD_STATE:(SSD_GROUPS + g + 1) * SSD_STATE]
        gs = slice(g * SSD_GROUP_WIDTH, (g + 1) * SSD_GROUP_WIDTH)
        st = state_ref[g]
        offs.append(jnp.dot(c_g, st.astype(BF16), preferred_element_type=F32))
        state_ref[g] = st * ea_full[q - 1:q, gs] + lax.dot_general(
            b_g, xw_b[:, gs], _TN, preferred_element_type=F32)
    y_off = jnp.concatenate(offs, axis=1) * ea_full

    y = y_diag + y_off + dfull_ref[...] * xs
    y = y * _silu(z_ref[...].astype(F32))
    normed = []
    for g in range(SSD_GROUPS):
        yg = y[:, g * SSD_GROUP_WIDTH:(g + 1) * SSD_GROUP_WIDTH]
        normed.append(yg * lax.rsqrt(jnp.mean(yg * yg, axis=-1, keepdims=True) + EPS))
    y_ref[...] = (jnp.concatenate(normed, axis=1) * normw_ref[...]).astype(BF16)


def _ssd(main, small, smallT, conv_w, conv_b, dt_bias, a_log, d_skip, ssd_norm_w, fox_f_bias, *, q):
    L = main.shape[0]
    expand = (jnp.arange(SSD_INNER)[None, :] // SSD_HEAD_DIM == jnp.arange(SSD_HEADS)[:, None]).astype(BF16)
    const = lambda shape: pl.BlockSpec(shape, lambda c: (0,) * len(shape))
    return pl.pallas_call(
        functools.partial(_ssd_kernel, q=q),
        grid=(L // q,),
        in_specs=[
            pl.BlockSpec((q, SSD_INNER), lambda c: (c, COL_Z // SSD_INNER)),
            pl.BlockSpec((q, SSD_INNER), lambda c: (c, COL_XS // SSD_INNER)),
            pl.BlockSpec((q, SSD_BC), lambda c: (c, COL_BC // SSD_BC)),
            pl.BlockSpec((q, SMALL_DIM), lambda c: (c, 0)),
            pl.BlockSpec((SMALL_DIM, q), lambda c: (0, c)),
            const((SSD_CONV, SSD_INNER)), const((1, SSD_INNER)),
            const((SSD_CONV, SSD_BC)), const((1, SSD_BC)),
            const((1, SSD_HEADS)), const((SSD_HEADS, 1)),
            const((1, SSD_HEADS)), const((SSD_HEADS, 1)),
            const((1, FOX_HEADS)),
            const((1, SSD_INNER)), const((1, SSD_INNER)),
            const((SSD_HEADS, SSD_INNER)),
        ],
        out_specs=[
            pl.BlockSpec((q, SSD_INNER), lambda c: (c, 0)),
            pl.BlockSpec((q, FOX_HEADS), lambda c: (c, 0)),
        ],
        out_shape=[
            jax.ShapeDtypeStruct((L, SSD_INNER), BF16),
            jax.ShapeDtypeStruct((L, FOX_HEADS), F32),
        ],
        scratch_shapes=[
            pltpu.VMEM((q + 8, SSD_INNER), F32),
            pltpu.VMEM((q + 8, SSD_BC), F32),
            pltpu.VMEM((SSD_GROUPS, SSD_STATE, SSD_GROUP_WIDTH), F32),
            pltpu.VMEM((1, FOX_HEADS), F32),
        ],
        compiler_params=pltpu.CompilerParams(
            dimension_semantics=("arbitrary",), vmem_limit_bytes=VMEM_LIMIT),
        name="ssd",
    )(main, main, main, small, smallT,
      conv_w[:, :SSD_INNER], conv_b[:SSD_INNER].reshape(1, -1),
      conv_w[:, SSD_INNER:], conv_b[SSD_INNER:].reshape(1, -1),
      dt_bias.reshape(1, -1), dt_bias.reshape(-1, 1),
      a_log.reshape(1, -1), a_log.reshape(-1, 1),
      fox_f_bias.reshape(1, -1),
      jnp.repeat(d_skip, SSD_HEAD_DIM).reshape(1, -1), ssd_norm_w.reshape(1, -1),
      expand)


FOX_AUG = 3


def _foxprep_kernel(q_ref, k_ref, v_ref, cum_ref, kaug_ref, qaugT_ref, vaugT_ref):
    p = pl.program_id(1)
    tb = q_ref.shape[0]
    lane = lax.broadcasted_iota(jnp.int32, (tb, LANES), 1)
    cum = cum_ref[...]
    rel = (cum[0:1, :] - cum) * LOG2E
    pieces = _split3(rel)
    hrow = lax.broadcasted_iota(jnp.int32, (FOX_HEADS, LANES), 0)
    hlane = lax.broadcasted_iota(jnp.int32, (FOX_HEADS, LANES), 1)
    q = q_ref[...].astype(F32) * (FOX_HEAD_DIM ** -0.5 * LOG2E)
    k = k_ref[...]
    v = v_ref[...].astype(F32)
    for hh in range(2):
        head_lanes = (lane < FOX_HEAD_DIM) if hh == 0 else (lane >= FOX_HEAD_DIM)
        base = FOX_HEAD_DIM if hh == 0 else 0
        h = 2 * p + hh
        aug = None
        for r in range(FOX_AUG):
            place = jnp.where(hrow == h, jnp.where(hlane == base + r, 1.0, 0.0), 0.0).astype(BF16)
            term = jnp.dot(pieces[r], place, preferred_element_type=F32)
            aug = term if aug is None else aug + term
        kaug_ref[hh] = jnp.where(head_lanes, k, aug.astype(BF16))
        is_aug = (lane >= base) & (lane < base + FOX_AUG)
        qaugT_ref[hh] = jnp.where(head_lanes, q, jnp.where(is_aug, 1.0, 0.0)).T.astype(BF16)
        vaugT_ref[hh] = jnp.where(head_lanes, v, jnp.where(lane == base, 1.0, 0.0)).T.astype(BF16)


def _foxprep(main, cum, *, tb):
    L = main.shape[0]
    pairs = FOX_HEADS // 2
    return pl.pallas_call(
        _foxprep_kernel,
        grid=(L // tb, pairs),
        in_specs=[
            pl.BlockSpec((tb, LANES), lambda b, p: (b, COL_Q // LANES + p)),
            pl.BlockSpec((tb, LANES), lambda b, p: (b, COL_K // LANES + p)),
            pl.BlockSpec((tb, LANES), lambda b, p: (b, COL_V // LANES + p)),
            pl.BlockSpec((tb, FOX_HEADS), lambda b, p: (b, 0)),
        ],
        out_specs=[
            pl.BlockSpec((2, tb, LANES), lambda b, p: (p, b, 0)),
            pl.BlockSpec((2, LANES, tb), lambda b, p: (p, 0, b)),
            pl.BlockSpec((2, LANES, tb), lambda b, p: (p, 0, b)),
        ],
        out_shape=[
            jax.ShapeDtypeStruct((FOX_HEADS, L, LANES), BF16),
            jax.ShapeDtypeStruct((FOX_HEADS, LANES, L), BF16),
            jax.ShapeDtypeStruct((FOX_HEADS, LANES, L), BF16),
        ],
        compiler_params=pltpu.CompilerParams(
            dimension_semantics=("parallel", "parallel"), vmem_limit_bytes=VMEM_LIMIT),
        name="foxprep",
    )(main, main, main, cum)


def _fox_kernel(ti_ref, tj_ref, cq_ref, ck_ref, kaug_ref, qaugT_ref, vaugT_ref, o_ref,
                m_ref, acc_ref, *, tq, tk):
    p = pl.program_id(0)
    n = pl.program_id(1)
    i = ti_ref[n]
    j = tj_ref[n]
    last = ((i + 1) * tq - 1) // tk

    @pl.when(j == 0)
    def _():
        m_ref[...] = jnp.full(m_ref.shape, NEG_BIG, F32)
        acc_ref[...] = jnp.zeros(acc_ref.shape, F32)

    def step(masked):
        if masked:
            key = j * tk + lax.broadcasted_iota(jnp.int32, (tk, tq), 0)
            qry = i * tq + lax.broadcasted_iota(jnp.int32, (tk, tq), 1)
            valid = key <= qry
        scores = [jnp.dot(kaug_ref[hh], qaugT_ref[hh], preferred_element_type=F32) for hh in range(2)]
        for hh in range(2):
            h = 2 * p + hh
            d = cq_ref[i * FOX_HEADS + h] - ck_ref[j * FOX_HEADS + h]
            sT = scores[hh]
            if masked:
                sT = jnp.where(valid, sT, NEG_BIG)
            m_prev = m_ref[hh]
            m_new = jnp.maximum(m_prev, jnp.max(sT, axis=0, keepdims=True) + d)
            alpha = jnp.exp2(m_prev - m_new)
            pT = jnp.exp2((sT - (m_new - d)).astype(BF16))
            acc_ref[hh] = alpha * acc_ref[hh] + jnp.dot(vaugT_ref[hh], pT, preferred_element_type=F32)
            m_ref[hh] = m_new

    crosses_diagonal = (j + 1) * tk - 1 > i * tq
    pl.when(crosses_diagonal)(lambda: step(True))
    pl.when(jnp.logical_not(crosses_diagonal))(lambda: step(False))

    @pl.when(j == last)
    def _():
        a0 = acc_ref[0]
        a1 = acc_ref[1]
        o0 = a0[0:FOX_HEAD_DIM, :] / a0[FOX_HEAD_DIM:FOX_HEAD_DIM + 1, :]
        o1 = a1[FOX_HEAD_DIM:, :] / a1[0:1, :]
        o_ref[...] = jnp.concatenate([o0, o1], axis=0).T.astype(BF16)


def _fox(main, cum, *, tq, tk):
    L = main.shape[0]
    nq, nk = L // tq, L // tk
    pairs = FOX_HEADS // 2
    kaug, qaugT, vaugT = _foxprep(main, cum, tb=tk)
    blocks = [(i, j) for i in range(nq) for j in range(((i + 1) * tq - 1) // tk + 1)]
    ti = jnp.asarray([b[0] for b in blocks], jnp.int32)
    tj = jnp.asarray([b[1] for b in blocks], jnp.int32)
    cq = (cum[::tq] * LOG2E).reshape(-1)
    ck = (cum[::tk] * LOG2E).reshape(-1)
    grid_spec = pltpu.PrefetchScalarGridSpec(
        num_scalar_prefetch=4,
        grid=(pairs, len(blocks)),
        in_specs=[
            pl.BlockSpec((2, tk, LANES), lambda p, n, ti, tj, cq, ck: (p, tj[n], 0)),
            pl.BlockSpec((2, LANES, tq), lambda p, n, ti, tj, cq, ck: (p, 0, ti[n])),
            pl.BlockSpec((2, LANES, tk), lambda p, n, ti, tj, cq, ck: (p, 0, tj[n])),
        ],
        out_specs=pl.BlockSpec((tq, LANES), lambda p, n, ti, tj, cq, ck: (ti[n], p)),
        scratch_shapes=[
            pltpu.VMEM((2, 1, tq), F32),
            pltpu.VMEM((2, LANES, tq), F32),
        ],
    )
    return pl.pallas_call(
        functools.partial(_fox_kernel, tq=tq, tk=tk),
        grid_spec=grid_spec,
        out_shape=jax.ShapeDtypeStruct((L, FOX_INNER), BF16),
        compiler_params=pltpu.CompilerParams(
            dimension_semantics=("parallel", "arbitrary"), vmem_limit_bytes=VMEM_LIMIT),
        name="fox",
    )(ti, tj, cq, ck, kaug, qaugT, vaugT)


def _mid_kernel(x_ref, ys_ref, yf_ref, wo_ref, ln2_ref, wq_ref, k1_ref, k2_ref,
                x1_ref, h2T_ref, s1T_ref, s2T_ref):
    mix = jnp.dot(ys_ref[...], wo_ref[0:SSD_INNER, :], preferred_element_type=F32)
    mix = mix + jnp.dot(yf_ref[...], wo_ref[SSD_INNER:, :], preferred_element_type=F32)
    x1 = x_ref[...] + mix
    x1_ref[...] = x1
    h2 = x1 * lax.rsqrt(jnp.mean(x1 * x1, axis=-1, keepdims=True) + EPS) * ln2_ref[...]
    h2T_ref[...] = h2.T.astype(BF16)
    qp = jnp.dot(h2.astype(BF16), wq_ref[...], preferred_element_type=F32)
    half = PEER_KEY_DIM // 2
    for h in range(PEER_HEADS):
        q1 = qp[:, h * PEER_KEY_DIM:h * PEER_KEY_DIM + half].astype(BF16)
        q2 = qp[:, h * PEER_KEY_DIM + half:(h + 1) * PEER_KEY_DIM].astype(BF16)
        s1T_ref[h] = lax.dot_general(k1_ref[h], q1, _NT, preferred_element_type=F32)
        s2T_ref[h] = lax.dot_general(k2_ref[h], q2, _NT, preferred_element_type=F32)


def _mid(x2d, y_ssd, y_fox, w_out, ln2_w, wq, k1, k2, *, tm):
    L = x2d.shape[0]
    half = PEER_KEY_DIM // 2
    resident = lambda shape: pl.BlockSpec(shape, lambda i: (0,) * len(shape), pipeline_mode=pl.Buffered(1))
    return pl.pallas_call(
        _mid_kernel,
        grid=(L // tm,),
        in_specs=[
            pl.BlockSpec((tm, D_MODEL), lambda i: (i, 0)),
            pl.BlockSpec((tm, SSD_INNER), lambda i: (i, 0)),
            pl.BlockSpec((tm, FOX_INNER), lambda i: (i, 0)),
            resident((D_MODEL, D_MODEL)),
            resident((1, D_MODEL)),
            resident((D_MODEL, PEER_HEADS * PEER_KEY_DIM)),
            resident((PEER_HEADS, PEER_N_KEYS, half)),
            resident((PEER_HEADS, PEER_N_KEYS, half)),
        ],
        out_specs=[
            pl.BlockSpec((tm, D_MODEL), lambda i: (i, 0)),
            pl.BlockSpec((D_MODEL, tm), lambda i: (0, i)),
            pl.BlockSpec((PEER_HEADS, PEER_N_KEYS, tm), lambda i: (0, 0, i)),
            pl.BlockSpec((PEER_HEADS, PEER_N_KEYS, tm), lambda i: (0, 0, i)),
        ],
        out_shape=[
            jax.ShapeDtypeStruct((L, D_MODEL), F32),
            jax.ShapeDtypeStruct((D_MODEL, L), BF16),
            jax.ShapeDtypeStruct((PEER_HEADS, PEER_N_KEYS, L), F32),
            jax.ShapeDtypeStruct((PEER_HEADS, PEER_N_KEYS, L), F32),
        ],
        compiler_params=pltpu.CompilerParams(
            dimension_semantics=("parallel",), vmem_limit_bytes=VMEM_LIMIT),
        name="mid",
    )(x2d, y_ssd, y_fox, w_out, ln2_w.reshape(1, D_MODEL), wq, k1, k2)


def _top16(s, key_id):
    t = s.shape[1]
    slot = lax.broadcasted_iota(jnp.int32, (PEER_TOPK, t), 0)
    n_sentinel = jnp.int32(1 << 20)

    def body(r, carry):
        s, rank, vals = carry
        m = jnp.max(s, axis=0, keepdims=True)
        idx = jnp.min(jnp.where(s == m, key_id, n_sentinel), axis=0, keepdims=True)
        hit = key_id == idx
        rank = jnp.where(hit, r.astype(F32), rank)
        s = jnp.where(hit, -jnp.inf, s)
        vals = jnp.where(slot == r, m, vals)
        return s, rank, vals

    init = (s, jnp.full(s.shape, float(PEER_TOPK), F32), jnp.zeros((PEER_TOPK, t), F32))
    _, rank, vals = lax.fori_loop(0, PEER_TOPK, body, init)
    return rank, vals


def _topk_kernel(s1T_ref, s2T_ref, lim_ref, e1_ref, r2_ref, e2_ref):
    t = s1T_ref.shape[2]
    key_id = lax.broadcasted_iota(jnp.int32, (PEER_N_KEYS, t), 0)
    half_k = PEER_TOPK // 2
    n_cand = half_k * PEER_TOPK + half_k
    pos_main = lax.broadcasted_iota(jnp.int32, (half_k * PEER_TOPK, t), 0)
    pos_tail = (lax.broadcasted_iota(jnp.int32, (half_k, t), 0) + half_k) * PEER_TOPK
    pos = jnp.concatenate([pos_main, pos_tail], axis=0)
    slot = lax.broadcasted_iota(jnp.int32, (PEER_TOPK, t), 0)
    n_sentinel = jnp.int32(1 << 20)

    def head(h, _):
        s1 = s1T_ref[h]
        s2 = s2T_ref[h]
        rank1, v1 = _top16(s1, key_id)
        rank2, v2 = _top16(s2, key_id)

        cand = jnp.concatenate(
            [v1[a:a + 1, :] + v2 for a in range(half_k)] + [v1[half_k:, :] + v2[0:1, :]], axis=0)
        top = v1[0:1, :] + v2[0:1, :]

        def body(r, carry):
            cand, sel, z = carry
            m = jnp.max(cand, axis=0, keepdims=True)
            idx = jnp.min(jnp.where(cand == m, pos, n_sentinel), axis=0, keepdims=True)
            hit = pos == idx
            sel = jnp.where(hit, 1.0, sel)
            cand = jnp.where(hit, -jnp.inf, cand)
            return cand, sel, z + jnp.exp(m - top)

        _, sel, z = lax.fori_loop(
            0, PEER_TOPK, body, (cand, jnp.zeros((n_cand, t), F32), jnp.zeros((1, t), F32)))

        bcount = jnp.zeros((PEER_TOPK, t), F32)
        for a in range(half_k):
            cnt = jnp.sum(sel[a * PEER_TOPK:(a + 1) * PEER_TOPK, :], axis=0, keepdims=True)
            bcount = jnp.where(slot == a, cnt, bcount)
        tail = jnp.concatenate([jnp.zeros((half_k, t), F32), sel[half_k * PEER_TOPK:, :]], axis=0)
        bcount = jnp.where(slot >= half_k, tail, bcount)

        lim = jnp.zeros((PEER_N_KEYS, t), F32)
        for a in range(PEER_TOPK):
            lim = jnp.where(rank1 == float(a), bcount[a:a + 1, :], lim)

        lim_ref[h] = lim
        e1_ref[h] = jnp.exp(s1 - v1[0:1, :])
        r2_ref[h] = rank2.astype(BF16)
        e2_ref[h] = (jnp.exp(s2 - v2[0:1, :]) / z).astype(BF16)
        return 0

    lax.fori_loop(0, PEER_HEADS, head, 0)


def _topk(s1T, s2T, *, tt):
    L = s1T.shape[2]
    spec = pl.BlockSpec((PEER_HEADS, PEER_N_KEYS, tt), lambda i: (0, 0, i))
    shape = jax.ShapeDtypeStruct((PEER_HEADS, PEER_N_KEYS, L), F32)
    shape_b = jax.ShapeDtypeStruct((PEER_HEADS, PEER_N_KEYS, L), BF16)
    return pl.pallas_call(
        _topk_kernel,
        grid=(L // tt,),
        in_specs=[spec, spec],
        out_specs=[spec] * 4,
        out_shape=[shape, shape, shape_b, shape_b],
        compiler_params=pltpu.CompilerParams(
            dimension_semantics=("parallel",), vmem_limit_bytes=VMEM_LIMIT),
        name="topk",
    )(s1T, s2T)


def _peer_kernel(h2T_ref, u_ref, vT_ref, lim_ref, e1_ref, r2_ref, e2_ref, x1_ref, lnf_ref,
                 o_ref, acc_ref, *, et, sub):
    e = pl.program_id(1)

    @pl.when(e == 0)
    def _():
        acc_ref[...] = jnp.zeros(acc_ref.shape, F32)

    keys_per_sub = sub // PEER_N_KEYS

    def activation(s):
        return jnp.dot(u_ref[s * sub:(s + 1) * sub, :], h2T_ref[...], preferred_element_type=F32)

    def gate(s):
        parts = []
        for i in range(keys_per_sub):
            i1 = e * (et // PEER_N_KEYS) + s * keys_per_sub + i
            w = None
            for h in range(PEER_HEADS):
                lim_row = lim_ref[h, pl.ds(i1, 1), :].astype(BF16)
                e1_row = e1_ref[h, pl.ds(i1, 1), :].astype(BF16)
                term = jnp.where(r2_ref[h] < lim_row, e2_ref[h] * e1_row, jnp.zeros((), BF16))
                w = term if w is None else w + term
            parts.append(w)
        return jnp.concatenate(parts, axis=0)

    total = None
    a = activation(0)
    for s in range(et // sub):
        a_next = activation(s + 1) if s + 1 < et // sub else None
        gelu = 0.5 * a * (1.0 + lax.erf(a * (2.0 ** -0.5)))
        wg = gelu.astype(BF16) * gate(s)
        part = jnp.dot(vT_ref[:, s * sub:(s + 1) * sub], wg, preferred_element_type=F32)
        total = part if total is None else total + part
        a = a_next
    acc_ref[...] += total

    @pl.when(e == pl.num_programs(1) - 1)
    def _():
        x2 = x1_ref[...] + acc_ref[...].T
        o_ref[...] = x2 * lax.rsqrt(jnp.mean(x2 * x2, axis=-1, keepdims=True) + EPS) * lnf_ref[...]


def _peer(h2T, u_b, vT_b, lim, e1, r2, e2, x1, lnf_w, *, tt, et, sub):
    L = x1.shape[0]
    tok = pl.BlockSpec((PEER_HEADS, PEER_N_KEYS, tt), lambda t, e: (0, 0, t), pipeline_mode=pl.Buffered(1))
    return pl.pallas_call(
        functools.partial(_peer_kernel, et=et, sub=sub),
        grid=(L // tt, PEER_N_EXPERTS // et),
        in_specs=[
            pl.BlockSpec((D_MODEL, tt), lambda t, e: (0, t), pipeline_mode=pl.Buffered(1)),
            pl.BlockSpec((et, D_MODEL), lambda t, e: (e, 0)),
            pl.BlockSpec((D_MODEL, et), lambda t, e: (0, e)),
            tok, tok, tok, tok,
            pl.BlockSpec((tt, D_MODEL), lambda t, e: (t, 0), pipeline_mode=pl.Buffered(1)),
            pl.BlockSpec((1, D_MODEL), lambda t, e: (0, 0)),
        ],
        out_specs=pl.BlockSpec((tt, D_MODEL), lambda t, e: (t, 0)),
        out_shape=jax.ShapeDtypeStruct((L, D_MODEL), F32),
        scratch_shapes=[
            pltpu.VMEM((D_MODEL, tt), F32),
        ],
        compiler_params=pltpu.CompilerParams(
            dimension_semantics=("parallel", "arbitrary"), vmem_limit_bytes=VMEM_LIMIT),
        name="peer",
    )(h2T, u_b, vT_b, lim, e1, r2, e2, x1, lnf_w.reshape(1, D_MODEL))


def _tiles(L):
    return dict(
        inproj_tm=min(512, L), inproj_tn=1408,
        ssd_q=min(128, L),
        fox_tq=min(1024, L), fox_tk=min(512, L),
        mid_tm=min(256, L),
        topk_tt=min(256, L),
        peer_tt=min(512, L), peer_et=1024, peer_sub=256,
    )


def kernel(x, ln1_w, w_in, conv_w, conv_b, dt_bias, a_log, d_skip, ssd_norm_w, fox_f_bias,
           w_out, ln2_w, peer_wq, peer_k1, peer_k2, peer_u, peer_v, lnf_w):
    B, L, D = x.shape
    assert B == 1 and D == D_MODEL
    t = _tiles(L)
    x2d = x.reshape(L, D)

    c_z, c_xbc, c_dt, c_q = 0, SSD_INNER, 2560, 2576
    c_f = c_q + 3 * FOX_INNER
    w_main = jnp.concatenate([w_in[:, c_z:c_dt], w_in[:, c_q:c_f]], axis=1).astype(BF16)
    w_small = jnp.concatenate(
        [w_in[:, c_dt:c_q], w_in[:, c_f:], jnp.zeros((D, SMALL_DIM - SSD_HEADS - FOX_HEADS), F32)],
        axis=1).astype(BF16)

    main, small = _inproj(x2d, ln1_w, w_main, w_small, tm=t["inproj_tm"], tn=t["inproj_tn"])
    y_ssd, cum = _ssd(main, small, small.T, conv_w, conv_b, dt_bias, a_log, d_skip, ssd_norm_w,
                      fox_f_bias, q=t["ssd_q"])
    y_fox = _fox(main, cum, tq=t["fox_tq"], tk=t["fox_tk"])
    x1, h2T, s1T, s2T = _mid(x2d, y_ssd, y_fox, w_out.astype(BF16), ln2_w, peer_wq.astype(BF16),
                             peer_k1.astype(BF16), peer_k2.astype(BF16), tm=t["mid_tm"])
    lim, e1, r2, e2 = _topk(s1T, s2T, tt=t["topk_tt"])
    out = _peer(h2T, peer_u.astype(BF16), peer_v.T.astype(BF16), lim, e1, r2, e2, x1, lnf_w,
                tt=t["peer_tt"], et=t["peer_et"], sub=t["peer_sub"])
    return out.reshape(B, L, D)
```

```python
import functools

import numpy as np
import jax
import jax.numpy as jnp
from jax import lax
from jax.experimental import pallas as pl
from jax.experimental.pallas import tpu as pltpu

F32 = jnp.float32
BF16 = jnp.bfloat16

D_MODEL = 2048
EPS = 1e-6

SSD_HEAD_DIM = 64
SSD_INNER = 1024
SSD_HEADS = 16
SSD_GROUPS = 2
SSD_STATE = 128
SSD_CONV = 4
SSD_BC = 2 * SSD_GROUPS * SSD_STATE
SSD_GROUP_WIDTH = SSD_INNER // SSD_GROUPS

FOX_HEAD_DIM = 64
FOX_INNER = 1024
FOX_HEADS = 16

PEER_HEADS = 8
PEER_N_KEYS = 128
PEER_N_EXPERTS = PEER_N_KEYS * PEER_N_KEYS
PEER_KEY_DIM = 256
PEER_TOPK = 16

COL_Z = 0
COL_XS = 1024
COL_BC = 2048
COL_Q = 2560
COL_K = 3584
COL_V = 4608
MAIN_DIM = 5632
SMALL_DIM = 128

LANES = 128
VMEM_LIMIT = 56 * 1024 * 1024

NEG_BIG = -1e30
LOG2E = 1.4426950408889634

_NT = (((1,), (1,)), ((), ()))
_TN = (((0,), (0,)), ((), ()))


def _softplus(x):
    return jnp.maximum(x, 0.0) + jnp.log1p(jnp.exp(-jnp.abs(x)))


def _silu(x):
    return x * jax.nn.sigmoid(x)


def _split3(x):
    hi = x.astype(BF16)
    r = x - hi.astype(F32)
    mid = r.astype(BF16)
    lo = (r - mid.astype(F32)).astype(BF16)
    return hi, mid, lo


def _dot3_lhs(x, w):
    a, b, c = _split3(x)
    d = lambda p: jnp.dot(p, w, preferred_element_type=F32)
    return d(a) + d(b) + d(c)


def _dot3_rhs(w, x):
    a, b, c = _split3(x)
    d = lambda p: jnp.dot(w, p, preferred_element_type=F32)
    return d(a) + d(b) + d(c)


def _inproj_kernel(x_ref, lnw_ref, wm_ref, ws_ref, main_ref, small_ref, h_ref):
    @pl.when(pl.program_id(1) == 0)
    def _():
        x = x_ref[...]
        h = x * lax.rsqrt(jnp.mean(x * x, axis=-1, keepdims=True) + EPS) * lnw_ref[...]
        hb = h.astype(BF16)
        h_ref[...] = hb
        small_ref[...] = jnp.dot(hb, ws_ref[...], preferred_element_type=F32)

    main_ref[...] = jnp.dot(h_ref[...], wm_ref[...], preferred_element_type=F32).astype(BF16)


def _inproj(x2d, ln1_w, w_main, w_small, *, tm, tn):
    L = x2d.shape[0]
    return pl.pallas_call(
        _inproj_kernel,
        grid=(L // tm, MAIN_DIM // tn),
        in_specs=[
            pl.BlockSpec((tm, D_MODEL), lambda i, j: (i, 0)),
            pl.BlockSpec((1, D_MODEL), lambda i, j: (0, 0)),
            pl.BlockSpec((D_MODEL, tn), lambda i, j: (0, j)),
            pl.BlockSpec((D_MODEL, SMALL_DIM), lambda i, j: (0, 0)),
        ],
        out_specs=[
            pl.BlockSpec((tm, tn), lambda i, j: (i, j)),
            pl.BlockSpec((tm, SMALL_DIM), lambda i, j: (i, 0)),
        ],
        out_shape=[
            jax.ShapeDtypeStruct((L, MAIN_DIM), BF16),
            jax.ShapeDtypeStruct((L, SMALL_DIM), F32),
        ],
        scratch_shapes=[pltpu.VMEM((tm, D_MODEL), BF16)],
        compiler_params=pltpu.CompilerParams(
            dimension_semantics=("parallel", "arbitrary"), vmem_limit_bytes=VMEM_LIMIT),
        name="inproj",
    )(x2d, ln1_w.reshape(1, D_MODEL), w_main, w_small)


def _ssd_kernel(z_ref, xs_ref, bc_ref, small_ref, smallT_ref,
                cwx_ref, cbx_ref, cwbc_ref, cbbc_ref,
                dtb_ref, dtbT_ref, alog_ref, alogT_ref, fb_ref,
                dfull_ref, normw_ref, expand_ref,
                y_ref, cum_ref,
                extx_ref, extbc_ref, state_ref, fcarry_ref, *, q):
    c = pl.program_id(0)
    tail = 8

    @pl.when(c == 0)
    def _():
        extx_ref[0:tail, :] = jnp.zeros((tail, SSD_INNER), F32)
        extbc_ref[0:tail, :] = jnp.zeros((tail, SSD_BC), F32)
        state_ref[...] = jnp.zeros(state_ref.shape, F32)
        fcarry_ref[...] = jnp.zeros(fcarry_ref.shape, F32)

    extx_ref[tail:tail + q, :] = xs_ref[...].astype(F32)
    extbc_ref[tail:tail + q, :] = bc_ref[...].astype(F32)

    def conv(ext_ref, w_ref, b_ref):
        acc = b_ref[...]
        for k in range(SSD_CONV):
            off = tail - (SSD_CONV - 1) + k
            acc = acc + ext_ref[off:off + q, :] * w_ref[k:k + 1, :]
        return acc

    xs = _silu(conv(extx_ref, cwx_ref, cbx_ref))
    bc = _silu(conv(extbc_ref, cwbc_ref, cbbc_ref))
    extx_ref[0:tail, :] = extx_ref[q:q + tail, :]
    extbc_ref[0:tail, :] = extbc_ref[q:q + tail, :]

    row = lax.broadcasted_iota(jnp.int32, (q, q), 0)
    col = lax.broadcasted_iota(jnp.int32, (q, q), 1)
    lower = row >= col
    tri = lower.astype(BF16)
    upper = (row <= col).astype(BF16)

    small = small_ref[...]
    smallT = smallT_ref[...]
    neg_a = -jnp.exp(alog_ref[...])
    neg_aT = -jnp.exp(alogT_ref[...])
    dt = _softplus(small[:, 0:SSD_HEADS] + dtb_ref[...])
    dtT = _softplus(smallT[0:SSD_HEADS, :] + dtbT_ref[...])
    a_cum = _dot3_rhs(tri, dt * neg_a)
    a_cumT = _dot3_lhs(dtT * neg_aT, upper)
    a_last = a_cum[q - 1:q, :]

    logf = -_softplus(-(small[:, SSD_HEADS:SSD_HEADS + FOX_HEADS] + fb_ref[...]))
    cumf = fcarry_ref[...] + _dot3_rhs(tri, logf)
    cum_ref[...] = cumf
    fcarry_ref[...] = cumf[q - 1:q, :]

    expand = expand_ref[...]
    dt_full = _dot3_lhs(dt, expand)
    ea_full = _dot3_lhs(jnp.exp(a_cum), expand)
    ds_full = _dot3_lhs(jnp.exp(a_last - a_cum), expand)

    xdt = xs * dt_full
    xdt_b = xdt.astype(BF16)
    xw_b = (xdt * ds_full).astype(BF16)
    bc_b = bc.astype(BF16)
    lane = lax.broadcasted_iota(jnp.int32, (q, LANES), 1)

    pieces = []
    for g in range(SSD_GROUPS):
        b_g = bc_b[:, g * SSD_STATE:(g + 1) * SSD_STATE]
        c_g = bc_b[:, (SSD_GROUPS + g) * SSD_STATE:(SSD_GROUPS + g + 1) * SSD_STATE]
        cb = lax.dot_general(c_g, b_g, _NT, preferred_element_type=F32)
        heads_per_group = SSD_HEADS // SSD_GROUPS
        for pp in range(heads_per_group // 2):
            pair = g * (heads_per_group // 2) + pp
            xp = xdt_b[:, pair * LANES:(pair + 1) * LANES]
            ys = []
            for hh in range(2):
                h = 2 * pair + hh
                seg = a_cum[:, h:h + 1] - a_cumT[h:h + 1, :]
                decay = jnp.exp(jnp.where(lower, seg, -jnp.inf))
                m = (cb * decay).astype(BF16)
                ys.append(jnp.dot(m, xp, preferred_element_type=F32))
            pieces.append(jnp.where(lane < SSD_HEAD_DIM, ys[0], ys[1]))
    y_diag = jnp.concatenate(pieces, axis=1)

    offs = []
    for g in range(SSD_GROUPS):
        b_g = bc_b[:, g * SSD_STATE:(g + 1) * SSD_STATE]
        c_g = bc_b[:, (SSD_GROUPS + g) * SSD_STATE:(SSD_GROUPS + g + 1) * SSD_STATE]
        gs = slice(g * SSD_GROUP_WIDTH, (g + 1) * SSD_GROUP_WIDTH)
        st = state_ref[g]
        offs.append(jnp.dot(c_g, st.astype(BF16), preferred_element_type=F32))
        state_ref[g] = st * ea_full[q - 1:q, gs] + lax.dot_general(
            b_g, xw_b[:, gs], _TN, preferred_element_type=F32)
    y_off = jnp.concatenate(offs, axis=1) * ea_full

    y = y_diag + y_off + dfull_ref[...] * xs
    y = y * _silu(z_ref[...].astype(F32))
    normed = []
    for g in range(SSD_GROUPS):
        yg = y[:, g * SSD_GROUP_WIDTH:(g + 1) * SSD_GROUP_WIDTH]
        normed.append(yg * lax.rsqrt(jnp.mean(yg * yg, axis=-1, keepdims=True) + EPS))
    y_ref[...] = (jnp.concatenate(normed, axis=1) * normw_ref[...]).astype(BF16)


def _ssd(main, small, smallT, conv_w, conv_b, dt_bias, a_log, d_skip, ssd_norm_w, fox_f_bias, *, q):
    L = main.shape[0]
    expand = (jnp.arange(SSD_INNER)[None, :] // SSD_HEAD_DIM == jnp.arange(SSD_HEADS)[:, None]).astype(BF16)
    const = lambda shape: pl.BlockSpec(shape, lambda c: (0,) * len(shape))
    return pl.pallas_call(
        functools.partial(_ssd_kernel, q=q),
        grid=(L // q,),
        in_specs=[
            pl.BlockSpec((q, SSD_INNER), lambda c: (c, COL_Z // SSD_INNER)),
            pl.BlockSpec((q, SSD_INNER), lambda c: (c, COL_XS // SSD_INNER)),
            pl.BlockSpec((q, SSD_BC), lambda c: (c, COL_BC // SSD_BC)),
            pl.BlockSpec((q, SMALL_DIM), lambda c: (c, 0)),
            pl.BlockSpec((SMALL_DIM, q), lambda c: (0, c)),
            const((SSD_CONV, SSD_INNER)), const((1, SSD_INNER)),
            const((SSD_CONV, SSD_BC)), const((1, SSD_BC)),
            const((1, SSD_HEADS)), const((SSD_HEADS, 1)),
            const((1, SSD_HEADS)), const((SSD_HEADS, 1)),
            const((1, FOX_HEADS)),
            const((1, SSD_INNER)), const((1, SSD_INNER)),
            const((SSD_HEADS, SSD_INNER)),
        ],
        out_specs=[
            pl.BlockSpec((q, SSD_INNER), lambda c: (c, 0)),
            pl.BlockSpec((q, FOX_HEADS), lambda c: (c, 0)),
        ],
        out_shape=[
            jax.ShapeDtypeStruct((L, SSD_INNER), BF16),
            jax.ShapeDtypeStruct((L, FOX_HEADS), F32),
        ],
        scratch_shapes=[
            pltpu.VMEM((q + 8, SSD_INNER), F32),
            pltpu.VMEM((q + 8, SSD_BC), F32),
            pltpu.VMEM((SSD_GROUPS, SSD_STATE, SSD_GROUP_WIDTH), F32),
            pltpu.VMEM((1, FOX_HEADS), F32),
        ],
        compiler_params=pltpu.CompilerParams(
            dimension_semantics=("arbitrary",), vmem_limit_bytes=VMEM_LIMIT),
        name="ssd",
    )(main, main, main, small, smallT,
      conv_w[:, :SSD_INNER], conv_b[:SSD_INNER].reshape(1, -1),
      conv_w[:, SSD_INNER:], conv_b[SSD_INNER:].reshape(1, -1),
      dt_bias.reshape(1, -1), dt_bias.reshape(-1, 1),
      a_log.reshape(1, -1), a_log.reshape(-1, 1),
      fox_f_bias.reshape(1, -1),
      jnp.repeat(d_skip, SSD_HEAD_DIM).reshape(1, -1), ssd_norm_w.reshape(1, -1),
      expand)


FOX_AUG = 3
FOX_SKIP_LOG2 = 160.0


def _foxprep_kernel(q_ref, k_ref, v_ref, cum_ref, kaug_ref, qaugT_ref, vaugT_ref, norm_ref):
    p = pl.program_id(1)
    stat_row = lax.broadcasted_iota(jnp.int32, (8, LANES), 0)
    stats = jnp.zeros((8, LANES), F32)
    tb = q_ref.shape[0]
    lane = lax.broadcasted_iota(jnp.int32, (tb, LANES), 1)
    cum = cum_ref[...]
    rel = (cum[0:1, :] - cum) * LOG2E
    pieces = _split3(rel)
    hrow = lax.broadcasted_iota(jnp.int32, (FOX_HEADS, LANES), 0)
    hlane = lax.broadcasted_iota(jnp.int32, (FOX_HEADS, LANES), 1)
    q = q_ref[...].astype(F32) * (FOX_HEAD_DIM ** -0.5 * LOG2E)
    k = k_ref[...]
    v = v_ref[...].astype(F32)
    for hh in range(2):
        head_lanes = (lane < FOX_HEAD_DIM) if hh == 0 else (lane >= FOX_HEAD_DIM)
        base = FOX_HEAD_DIM if hh == 0 else 0
        h = 2 * p + hh
        aug = None
        for r in range(FOX_AUG):
            place = jnp.where(hrow == h, jnp.where(hlane == base + r, 1.0, 0.0), 0.0).astype(BF16)
            term = jnp.dot(pieces[r], place, preferred_element_type=F32)
            aug = term if aug is None else aug + term
        kaug_ref[hh] = jnp.where(head_lanes, k, aug.astype(BF16))
        is_aug = (lane >= base) & (lane < base + FOX_AUG)
        qa = jnp.where(head_lanes, q, jnp.where(is_aug, 1.0, 0.0)).astype(BF16)
        qaugT_ref[hh] = qa.astype(F32).T.astype(BF16)
        vaugT_ref[hh] = jnp.where(head_lanes, v, jnp.where(lane == base, 1.0, 0.0)).T.astype(BF16)
        qh = jnp.where(head_lanes, qa.astype(F32), 0.0)
        kh = jnp.where(head_lanes, k.astype(F32), 0.0)
        qn2 = jnp.max(jnp.sum(qh * qh, axis=1, keepdims=True), axis=0, keepdims=True)
        kn2 = jnp.max(jnp.sum(kh * kh, axis=1, keepdims=True), axis=0, keepdims=True)
        stats = jnp.where(stat_row == hh, qn2, jnp.where(stat_row == 2 + hh, kn2, stats))
    norm_ref[...] = stats


def _foxprep(main, cum, *, tb):
    L = main.shape[0]
    pairs = FOX_HEADS // 2
    return pl.pallas_call(
        _foxprep_kernel,
        grid=(L // tb, pairs),
        in_specs=[
            pl.BlockSpec((tb, LANES), lambda b, p: (b, COL_Q // LANES + p)),
            pl.BlockSpec((tb, LANES), lambda b, p: (b, COL_K // LANES + p)),
            pl.BlockSpec((tb, LANES), lambda b, p: (b, COL_V // LANES + p)),
            pl.BlockSpec((tb, FOX_HEADS), lambda b, p: (b, 0)),
        ],
        out_specs=[
            pl.BlockSpec((2, tb, LANES), lambda b, p: (p, b, 0)),
            pl.BlockSpec((2, LANES, tb), lambda b, p: (p, 0, b)),
            pl.BlockSpec((2, LANES, tb), lambda b, p: (p, 0, b)),
            pl.BlockSpec((None, None, 8, LANES), lambda b, p: (b, p, 0, 0)),
        ],
        out_shape=[
            jax.ShapeDtypeStruct((FOX_HEADS, L, LANES), BF16),
            jax.ShapeDtypeStruct((FOX_HEADS, LANES, L), BF16),
            jax.ShapeDtypeStruct((FOX_HEADS, LANES, L), BF16),
            jax.ShapeDtypeStruct((L // tb, pairs, 8, LANES), F32),
        ],
        compiler_params=pltpu.CompilerParams(
            dimension_semantics=("parallel", "parallel"), vmem_limit_bytes=VMEM_LIMIT),
        name="foxprep",
    )(main, main, main, cum)


def _fox_kernel(tp_ref, ti_ref, tj_ref, first_ref, count_ref, cq_ref, ck_ref,
                kaug_ref, qaugT_ref, vaugT_ref, o_ref, m_ref, acc_ref, *, tq, tk):
    n = pl.program_id(0)
    p = tp_ref[n]
    i = ti_ref[n]
    j = tj_ref[n]
    live = n < count_ref[0]
    last = ((i + 1) * tq - 1) // tk

    @pl.when(jnp.logical_and(live, first_ref[n] == 1))
    def _():
        m_ref[...] = jnp.full(m_ref.shape, NEG_BIG, F32)
        acc_ref[...] = jnp.zeros(acc_ref.shape, F32)

    def step(masked):
        if masked:
            key = j * tk + lax.broadcasted_iota(jnp.int32, (tk, tq), 0)
            qry = i * tq + lax.broadcasted_iota(jnp.int32, (tk, tq), 1)
            valid = key <= qry
        scores = [jnp.dot(kaug_ref[hh], qaugT_ref[hh], preferred_element_type=F32) for hh in range(2)]
        for hh in range(2):
            h = 2 * p + hh
            d = cq_ref[i * FOX_HEADS + h] - ck_ref[j * FOX_HEADS + h]
            sT = scores[hh]
            if masked:
                sT = jnp.where(valid, sT, NEG_BIG)
            m_prev = m_ref[hh]
            m_new = jnp.maximum(m_prev, jnp.max(sT, axis=0, keepdims=True) + d)
            alpha = jnp.exp2(m_prev - m_new)
            pT = jnp.exp2((sT - (m_new - d)).astype(BF16))
            acc_ref[hh] = alpha * acc_ref[hh] + jnp.dot(vaugT_ref[hh], pT, preferred_element_type=F32)
            m_ref[hh] = m_new

    crosses_diagonal = (j + 1) * tk - 1 > i * tq
    pl.when(jnp.logical_and(live, crosses_diagonal))(lambda: step(True))
    pl.when(jnp.logical_and(live, jnp.logical_not(crosses_diagonal)))(lambda: step(False))

    @pl.when(jnp.logical_and(live, j == last))
    def _():
        a0 = acc_ref[0]
        a1 = acc_ref[1]
        o0 = a0[0:FOX_HEAD_DIM, :] / a0[FOX_HEAD_DIM:FOX_HEAD_DIM + 1, :]
        o1 = a1[FOX_HEAD_DIM:, :] / a1[0:1, :]
        o_ref[...] = jnp.concatenate([o0, o1], axis=0).T.astype(BF16)


def _fox_tables(cum, norms, *, tq, tk):
    L = cum.shape[0]
    nq, nk = L // tq, L // tk
    pairs = FOX_HEADS // 2
    assert nq < 256 and nk < 256
    pp, ii, jj = np.meshgrid(np.arange(pairs), np.arange(nq), np.arange(nk), indexing="ij")
    causal = jj * tk <= (ii + 1) * tq - 1
    diagonal = (jj + 1) * tk - 1 > ii * tq
    code = jnp.asarray((pp << 16 | ii << 8 | jj).reshape(-1), jnp.int32)
    n_steps = int(causal.sum())

    qn = jnp.sqrt(norms[:, :, 0:2, 0]).reshape(nk, FOX_HEADS)
    kn = jnp.sqrt(norms[:, :, 2:4, 0]).reshape(nk, FOX_HEADS)
    qn_i = qn.reshape(nq, tq // tk, FOX_HEADS).max(axis=1)
    kd_i = kn.reshape(nq, tq // tk, FOX_HEADS).max(axis=1)
    cq = cum[::tq] * LOG2E
    ck = cum[::tk] * LOG2E
    c_last = cum[tk - 1::tk] * LOG2E
    bound = qn_i[:, None, :] * (kn[None, :, :] + kd_i[:, None, :]) + (cq[:, None, :] - c_last[None, :, :])
    alive = (bound >= -FOX_SKIP_LOG2).reshape(nq, nk, pairs, 2).any(axis=-1)
    live = (jnp.asarray(causal) & (jnp.asarray(diagonal) | jnp.transpose(alive, (2, 0, 1)))).reshape(-1)
    pos = jnp.cumsum(live.astype(jnp.int32)) - 1
    count = pos[-1] + 1
    slot = jnp.arange(n_steps, dtype=jnp.int32)
    hit = live[None, :] & (pos[None, :] == slot[:, None])
    picked = jnp.sum(jnp.where(hit, code[None, :], 0), axis=1)
    picked = jnp.where(slot < count, picked, jnp.max(jnp.where(live, code, -1)))
    tp, ti, tj = picked >> 16, (picked >> 8) & 255, picked & 255
    first = jnp.concatenate([jnp.ones((1,), jnp.int32),
                             ((tp[1:] != tp[:-1]) | (ti[1:] != ti[:-1])).astype(jnp.int32)])
    return tp, ti, tj, first, count.reshape(1), cq.reshape(-1), ck.reshape(-1)


def _fox(main, cum, *, tq, tk):
    L = main.shape[0]
    kaug, qaugT, vaugT, norms = _foxprep(main, cum, tb=tk)
    tables = _fox_tables(cum, norms, tq=tq, tk=tk)
    grid_spec = pltpu.PrefetchScalarGridSpec(
        num_scalar_prefetch=len(tables),
        grid=(tables[0].shape[0],),
        in_specs=[
            pl.BlockSpec((2, tk, LANES), lambda n, tp, ti, tj, *_: (tp[n], tj[n], 0)),
            pl.BlockSpec((2, LANES, tq), lambda n, tp, ti, tj, *_: (tp[n], 0, ti[n])),
            pl.BlockSpec((2, LANES, tk), lambda n, tp, ti, tj, *_: (tp[n], 0, tj[n])),
        ],
        out_specs=pl.BlockSpec((tq, LANES), lambda n, tp, ti, tj, *_: (ti[n], tp[n])),
        scratch_shapes=[
            pltpu.VMEM((2, 1, tq), F32),
            pltpu.VMEM((2, LANES, tq), F32),
        ],
    )
    return pl.pallas_call(
        functools.partial(_fox_kernel, tq=tq, tk=tk),
        grid_spec=grid_spec,
        out_shape=jax.ShapeDtypeStruct((L, FOX_INNER), BF16),
        compiler_params=pltpu.CompilerParams(
            dimension_semantics=("arbitrary",), vmem_limit_bytes=VMEM_LIMIT),
        name="fox",
    )(*tables, kaug, qaugT, vaugT)


def _mid_kernel(x_ref, ys_ref, yf_ref, wo_ref, ln2_ref, wq_ref, k1_ref, k2_ref,
                x1_ref, h2T_ref, s1T_ref, s2T_ref):
    mix = jnp.dot(ys_ref[...], wo_ref[0:SSD_INNER, :], preferred_element_type=F32)
    mix = mix + jnp.dot(yf_ref[...], wo_ref[SSD_INNER:, :], preferred_element_type=F32)
    x1 = x_ref[...] + mix
    x1_ref[...] = x1
    h2 = x1 * lax.rsqrt(jnp.mean(x1 * x1, axis=-1, keepdims=True) + EPS) * ln2_ref[...]
    h2T_ref[...] = h2.T.astype(BF16)
    qp = jnp.dot(h2.astype(BF16), wq_ref[...], preferred_element_type=F32)
    half = PEER_KEY_DIM // 2
    for h in range(PEER_HEADS):
        q1 = qp[:, h * PEER_KEY_DIM:h * PEER_KEY_DIM + half].astype(BF16)
        q2 = qp[:, h * PEER_KEY_DIM + half:(h + 1) * PEER_KEY_DIM].astype(BF16)
        s1T_ref[h] = lax.dot_general(k1_ref[h], q1, _NT, preferred_element_type=F32)
        s2T_ref[h] = lax.dot_general(k2_ref[h], q2, _NT, preferred_element_type=F32)


def _mid(x2d, y_ssd, y_fox, w_out, ln2_w, wq, k1, k2, *, tm):
    L = x2d.shape[0]
    half = PEER_KEY_DIM // 2
    resident = lambda shape: pl.BlockSpec(shape, lambda i: (0,) * len(shape), pipeline_mode=pl.Buffered(1))
    return pl.pallas_call(
        _mid_kernel,
        grid=(L // tm,),
        in_specs=[
            pl.BlockSpec((tm, D_MODEL), lambda i: (i, 0)),
            pl.BlockSpec((tm, SSD_INNER), lambda i: (i, 0)),
            pl.BlockSpec((tm, FOX_INNER), lambda i: (i, 0)),
            resident((D_MODEL, D_MODEL)),
            resident((1, D_MODEL)),
            resident((D_MODEL, PEER_HEADS * PEER_KEY_DIM)),
            resident((PEER_HEADS, PEER_N_KEYS, half)),
            resident((PEER_HEADS, PEER_N_KEYS, half)),
        ],
        out_specs=[
            pl.BlockSpec((tm, D_MODEL), lambda i: (i, 0)),
            pl.BlockSpec((D_MODEL, tm), lambda i: (0, i)),
            pl.BlockSpec((PEER_HEADS, PEER_N_KEYS, tm), lambda i: (0, 0, i)),
            pl.BlockSpec((PEER_HEADS, PEER_N_KEYS, tm), lambda i: (0, 0, i)),
        ],
        out_shape=[
            jax.ShapeDtypeStruct((L, D_MODEL), F32),
            jax.ShapeDtypeStruct((D_MODEL, L), BF16),
            jax.ShapeDtypeStruct((PEER_HEADS, PEER_N_KEYS, L), F32),
            jax.ShapeDtypeStruct((PEER_HEADS, PEER_N_KEYS, L), F32),
        ],
        compiler_params=pltpu.CompilerParams(
            dimension_semantics=("parallel",), vmem_limit_bytes=VMEM_LIMIT),
        name="mid",
    )(x2d, y_ssd, y_fox, w_out, ln2_w.reshape(1, D_MODEL), wq, k1, k2)


def _top16(s, order, exact):
    axis = s.ndim - 2
    slot = lax.broadcasted_iota(jnp.int32, s.shape[:-2] + (PEER_TOPK, s.shape[-1]), axis)
    unranked = float(PEER_TOPK)

    def body(r, carry):
        s, rank, vals = carry
        m = jnp.max(s, axis=axis, keepdims=True)
        hit = s == m
        if exact:
            first = jnp.min(jnp.where(hit, order, jnp.inf), axis=axis, keepdims=True)
            hit = order == first
            rank = jnp.where(hit, lax.convert_element_type(r, F32), rank)
        s = jnp.where(hit, -jnp.inf, s)
        vals = jnp.where(slot == r, m, vals)
        return s, rank, vals

    rank0 = jnp.full(s.shape if exact else (1, 1), unranked, F32)
    left, rank, vals = lax.fori_loop(0, PEER_TOPK, body, (s, rank0, jnp.zeros(slot.shape, F32)))
    taken = (rank < unranked) if exact else (left == -jnp.inf)
    n_taken = jnp.sum(jnp.where(taken, 1.0, 0.0), axis=axis, keepdims=True)
    return (rank if exact else None), vals, taken, n_taken


def _topk_kernel(s1T_ref, s2T_ref, lim_ref, e1_ref, r2_ref, e2_ref):
    t = s1T_ref.shape[2]
    key_id = lax.broadcasted_iota(jnp.int32, (PEER_N_KEYS, t), 0).astype(F32)
    half_k = PEER_TOPK // 2
    pos_main = lax.broadcasted_iota(jnp.int32, (half_k * PEER_TOPK, t), 0)
    pos_tail = (lax.broadcasted_iota(jnp.int32, (half_k, t), 0) + half_k) * PEER_TOPK
    pos = jnp.concatenate([pos_main, pos_tail], axis=0).astype(F32)
    slot = lax.broadcasted_iota(jnp.int32, (PEER_TOPK, t), 0)

    def solve(h, exact):
        s1 = s1T_ref[h]
        s2 = s2T_ref[h]
        rank12, v12, _, n12 = _top16(jnp.stack([s1, s2]), key_id, exact)
        v1, v2 = v12[0], v12[1]

        cand = jnp.concatenate(
            [v1[a:a + 1, :] + v2 for a in range(half_k)] + [v1[half_k:, :] + v2[0:1, :]], axis=0)
        _, top_sums, taken_c, n_c = _top16(cand, pos, exact)
        sel = jnp.where(taken_c, 1.0, 0.0)
        z = jnp.sum(jnp.exp(top_sums - top_sums[0:1, :]), axis=0, keepdims=True)
        n_taken = jnp.maximum(jnp.maximum(n12[0], n12[1]), n_c)

        if exact:
            rank1, rank2 = rank12[0], rank12[1]
            is_rank1 = lambda a: rank1 == float(a)
        else:
            rank2 = jnp.full(s2.shape, float(PEER_TOPK), F32)
            for a in range(PEER_TOPK):
                rank2 = jnp.where(s2 == v2[a:a + 1, :], float(a), rank2)
            is_rank1 = lambda a: s1 == v1[a:a + 1, :]

        bcount = jnp.zeros((PEER_TOPK, t), F32)
        for a in range(half_k):
            cnt = jnp.sum(sel[a * PEER_TOPK:(a + 1) * PEER_TOPK, :], axis=0, keepdims=True)
            bcount = jnp.where(slot == a, cnt, bcount)
        tail = jnp.concatenate([jnp.zeros((half_k, t), F32), sel[half_k * PEER_TOPK:, :]], axis=0)
        bcount = jnp.where(slot >= half_k, tail, bcount)

        lim = jnp.zeros((PEER_N_KEYS, t), F32)
        for a in range(PEER_TOPK):
            lim = jnp.where(is_rank1(a), bcount[a:a + 1, :], lim)

        lim_ref[h] = lim
        e1_ref[h] = jnp.exp(s1 - v1[0:1, :])
        r2_ref[h] = rank2.astype(BF16)
        e2_ref[h] = (jnp.exp(s2 - v2[0:1, :]) / z).astype(BF16)
        return jnp.max(n_taken)

    def head(h, _):
        most = solve(h, exact=False)

        @pl.when(most > float(PEER_TOPK))
        def _():
            solve(h, exact=True)

        return 0

    lax.fori_loop(0, PEER_HEADS, head, 0)


def _topk(s1T, s2T, *, tt):
    L = s1T.shape[2]
    spec = pl.BlockSpec((PEER_HEADS, PEER_N_KEYS, tt), lambda i: (0, 0, i))
    shape = jax.ShapeDtypeStruct((PEER_HEADS, PEER_N_KEYS, L), F32)
    shape_b = jax.ShapeDtypeStruct((PEER_HEADS, PEER_N_KEYS, L), BF16)
    return pl.pallas_call(
        _topk_kernel,
        grid=(L // tt,),
        in_specs=[spec, spec],
        out_specs=[spec] * 4,
        out_shape=[shape, shape, shape_b, shape_b],
        compiler_params=pltpu.CompilerParams(
            dimension_semantics=("parallel",), vmem_limit_bytes=VMEM_LIMIT),
        name="topk",
    )(s1T, s2T)


def _peer_kernel(h2T_ref, u_ref, vT_ref, lim_ref, e1_ref, r2_ref, e2_ref, x1_ref, lnf_ref,
                 o_ref, acc_ref, *, et, sub):
    e = pl.program_id(1)

    @pl.when(e == 0)
    def _():
        acc_ref[...] = jnp.zeros(acc_ref.shape, F32)

    keys_per_sub = sub // PEER_N_KEYS

    def activation(s):
        return jnp.dot(u_ref[s * sub:(s + 1) * sub, :], h2T_ref[...], preferred_element_type=F32)

    def gate(s):
        parts = []
        for i in range(keys_per_sub):
            i1 = e * (et // PEER_N_KEYS) + s * keys_per_sub + i
            w = None
            for h in range(PEER_HEADS):
                lim_row = lim_ref[h, pl.ds(i1, 1), :].astype(BF16)
                e1_row = e1_ref[h, pl.ds(i1, 1), :].astype(BF16)
                term = jnp.where(r2_ref[h] < lim_row, e2_ref[h] * e1_row, jnp.zeros((), BF16))
                w = term if w is None else w + term
            parts.append(w)
        return jnp.concatenate(parts, axis=0)

    total = None
    a = activation(0)
    for s in range(et // sub):
        a_next = activation(s + 1) if s + 1 < et // sub else None
        gelu = 0.5 * a * (1.0 + lax.erf(a * (2.0 ** -0.5)))
        wg = gelu.astype(BF16) * gate(s)
        part = jnp.dot(vT_ref[:, s * sub:(s + 1) * sub], wg, preferred_element_type=F32)
        total = part if total is None else total + part
        a = a_next
    acc_ref[...] += total

    @pl.when(e == pl.num_programs(1) - 1)
    def _():
        x2 = x1_ref[...] + acc_ref[...].T
        o_ref[...] = x2 * lax.rsqrt(jnp.mean(x2 * x2, axis=-1, keepdims=True) + EPS) * lnf_ref[...]


def _peer(h2T, u_b, vT_b, lim, e1, r2, e2, x1, lnf_w, *, tt, et, sub):
    L = x1.shape[0]
    tok = pl.BlockSpec((PEER_HEADS, PEER_N_KEYS, tt), lambda t, e: (0, 0, t), pipeline_mode=pl.Buffered(1))
    return pl.pallas_call(
        functools.partial(_peer_kernel, et=et, sub=sub),
        grid=(L // tt, PEER_N_EXPERTS // et),
        in_specs=[
            pl.BlockSpec((D_MODEL, tt), lambda t, e: (0, t), pipeline_mode=pl.Buffered(1)),
            pl.BlockSpec((et, D_MODEL), lambda t, e: (e, 0)),
            pl.BlockSpec((D_MODEL, et), lambda t, e: (0, e)),
            tok, tok, tok, tok,
            pl.BlockSpec((tt, D_MODEL), lambda t, e: (t, 0), pipeline_mode=pl.Buffered(1)),
            pl.BlockSpec((1, D_MODEL), lambda t, e: (0, 0)),
        ],
        out_specs=pl.BlockSpec((tt, D_MODEL), lambda t, e: (t, 0)),
        out_shape=jax.ShapeDtypeStruct((L, D_MODEL), F32),
        scratch_shapes=[
            pltpu.VMEM((D_MODEL, tt), F32),
        ],
        compiler_params=pltpu.CompilerParams(
            dimension_semantics=("parallel", "arbitrary"), vmem_limit_bytes=VMEM_LIMIT),
        name="peer",
    )(h2T, u_b, vT_b, lim, e1, r2, e2, x1, lnf_w.reshape(1, D_MODEL))


def _tiles(L):
    return dict(
        inproj_tm=min(512, L), inproj_tn=1408,
        ssd_q=min(128, L),
        fox_tq=min(1024, L), fox_tk=min(512, L),
        mid_tm=min(256, L),
        topk_tt=min(256, L),
        peer_tt=min(512, L), peer_et=1024, peer_sub=256,
    )


def kernel(x, ln1_w, w_in, conv_w, conv_b, dt_bias, a_log, d_skip, ssd_norm_w, fox_f_bias,
           w_out, ln2_w, peer_wq, peer_k1, peer_k2, peer_u, peer_v, lnf_w):
    B, L, D = x.shape
    assert B == 1 and D == D_MODEL
    t = _tiles(L)
    x2d = x.reshape(L, D)

    c_z, c_xbc, c_dt, c_q = 0, SSD_INNER, 2560, 2576
    c_f = c_q + 3 * FOX_INNER
    w_main = jnp.concatenate([w_in[:, c_z:c_dt], w_in[:, c_q:c_f]], axis=1).astype(BF16)
    w_small = jnp.concatenate(
        [w_in[:, c_dt:c_q], w_in[:, c_f:], jnp.zeros((D, SMALL_DIM - SSD_HEADS - FOX_HEADS), F32)],
        axis=1).astype(BF16)

    main, small = _inproj(x2d, ln1_w, w_main, w_small, tm=t["inproj_tm"], tn=t["inproj_tn"])
    y_ssd, cum = _ssd(main, small, small.T, conv_w, conv_b, dt_bias, a_log, d_skip, ssd_norm_w,
                      fox_f_bias, q=t["ssd_q"])
    y_fox = _fox(main, cum, tq=t["fox_tq"], tk=t["fox_tk"])
    x1, h2T, s1T, s2T = _mid(x2d, y_ssd, y_fox, w_out.astype(BF16), ln2_w, peer_wq.astype(BF16),
                             peer_k1.astype(BF16), peer_k2.astype(BF16), tm=t["mid_tm"])
    lim, e1, r2, e2 = _topk(s1T, s2T, tt=t["topk_tt"])
    out = _peer(h2T, peer_u.astype(BF16), peer_v.T.astype(BF16), lim, e1, r2, e2, x1, lnf_w,
                tt=t["peer_tt"], et=t["peer_et"], sub=t["peer_sub"])
    return out.reshape(B, L, D)
```

```python
import functools

import numpy as np
import jax
import jax.numpy as jnp
from jax import lax
from jax.experimental import pallas as pl
from jax.experimental.pallas import tpu as pltpu

F32 = jnp.float32
BF16 = jnp.bfloat16

D_MODEL = 2048
EPS = 1e-6

SSD_HEAD_DIM = 64
SSD_INNER = 1024
SSD_HEADS = 16
SSD_GROUPS = 2
SSD_STATE = 128
SSD_CONV = 4
SSD_BC = 2 * SSD_GROUPS * SSD_STATE
SSD_GROUP_WIDTH = SSD_INNER // SSD_GROUPS

FOX_HEAD_DIM = 64
FOX_INNER = 1024
FOX_HEADS = 16

PEER_HEADS = 8
PEER_N_KEYS = 128
PEER_N_EXPERTS = PEER_N_KEYS * PEER_N_KEYS
PEER_KEY_DIM = 256
PEER_TOPK = 16

COL_Z = 0
COL_XS = 1024
COL_BC = 2048
COL_Q = 2560
COL_K = 3584
COL_V = 4608
MAIN_DIM = 5632
SMALL_DIM = 128

LANES = 128
VMEM_LIMIT = 56 * 1024 * 1024

NEG_BIG = -1e30
LOG2E = 1.4426950408889634

_NT = (((1,), (1,)), ((), ()))
_TN = (((0,), (0,)), ((), ()))


def _softplus(x):
    return jnp.maximum(x, 0.0) + jnp.log1p(jnp.exp(-jnp.abs(x)))


def _silu(x):
    return x * jax.nn.sigmoid(x)


def _split3(x):
    hi = x.astype(BF16)
    r = x - hi.astype(F32)
    mid = r.astype(BF16)
    lo = (r - mid.astype(F32)).astype(BF16)
    return hi, mid, lo


def _dot3_lhs(x, w):
    a, b, c = _split3(x)
    d = lambda p: jnp.dot(p, w, preferred_element_type=F32)
    return d(a) + d(b) + d(c)


def _dot3_rhs(w, x):
    a, b, c = _split3(x)
    d = lambda p: jnp.dot(w, p, preferred_element_type=F32)
    return d(a) + d(b) + d(c)


def _inproj_kernel(x_ref, lnw_ref, wm_ref, ws_ref, wsT_ref, main_ref, small_ref, smallT_ref, h_ref):
    @pl.when(pl.program_id(1) == 0)
    def _():
        x = x_ref[...]
        h = x * lax.rsqrt(jnp.mean(x * x, axis=-1, keepdims=True) + EPS) * lnw_ref[...]
        hb = h.astype(BF16)
        h_ref[...] = hb
        small_ref[...] = jnp.dot(hb, ws_ref[...], preferred_element_type=F32)
        smallT_ref[...] = lax.dot_general(wsT_ref[...], hb, _NT, preferred_element_type=F32)

    main_ref[...] = jnp.dot(h_ref[...], wm_ref[...], preferred_element_type=F32).astype(BF16)


def _inproj(x2d, ln1_w, w_main, w_small, *, tm, tn):
    L = x2d.shape[0]
    return pl.pallas_call(
        _inproj_kernel,
        grid=(L // tm, MAIN_DIM // tn),
        in_specs=[
            pl.BlockSpec((tm, D_MODEL), lambda i, j: (i, 0)),
            pl.BlockSpec((1, D_MODEL), lambda i, j: (0, 0)),
            pl.BlockSpec((D_MODEL, tn), lambda i, j: (0, j)),
            pl.BlockSpec((D_MODEL, SMALL_DIM), lambda i, j: (0, 0)),
            pl.BlockSpec((SMALL_DIM, D_MODEL), lambda i, j: (0, 0)),
        ],
        out_specs=[
            pl.BlockSpec((tm, tn), lambda i, j: (i, j)),
            pl.BlockSpec((tm, SMALL_DIM), lambda i, j: (i, 0)),
            pl.BlockSpec((SMALL_DIM, tm), lambda i, j: (0, i)),
        ],
        out_shape=[
            jax.ShapeDtypeStruct((L, MAIN_DIM), BF16),
            jax.ShapeDtypeStruct((L, SMALL_DIM), F32),
            jax.ShapeDtypeStruct((SMALL_DIM, L), F32),
        ],
        scratch_shapes=[pltpu.VMEM((tm, D_MODEL), BF16)],
        compiler_params=pltpu.CompilerParams(
            dimension_semantics=("parallel", "arbitrary"), vmem_limit_bytes=VMEM_LIMIT),
        name="inproj",
    )(x2d, ln1_w.reshape(1, D_MODEL), w_main, w_small, w_small.T)


def _ssd_kernel(z_ref, xs_ref, bc_ref, small_ref, smallT_ref,
                cwx_ref, cbx_ref, cwbc_ref, cbbc_ref,
                dtb_ref, dtbT_ref, alog_ref, alogT_ref, fb_ref,
                dfull_ref, normw_ref, expand_ref,
                y_ref, cum_ref,
                extx_ref, extbc_ref, state_ref, fcarry_ref, *, q):
    c = pl.program_id(0)
    tail = 8

    @pl.when(c == 0)
    def _():
        extx_ref[0:tail, :] = jnp.zeros((tail, SSD_INNER), F32)
        extbc_ref[0:tail, :] = jnp.zeros((tail, SSD_BC), F32)
        state_ref[...] = jnp.zeros(state_ref.shape, F32)
        fcarry_ref[...] = jnp.zeros(fcarry_ref.shape, F32)

    extx_ref[tail:tail + q, :] = xs_ref[...].astype(F32)
    extbc_ref[tail:tail + q, :] = bc_ref[...].astype(F32)

    def conv(ext_ref, w_ref, b_ref):
        acc = b_ref[...]
        for k in range(SSD_CONV):
            off = tail - (SSD_CONV - 1) + k
            acc = acc + ext_ref[off:off + q, :] * w_ref[k:k + 1, :]
        return acc

    xs = _silu(conv(extx_ref, cwx_ref, cbx_ref))
    bc = _silu(conv(extbc_ref, cwbc_ref, cbbc_ref))
    extx_ref[0:tail, :] = extx_ref[q:q + tail, :]
    extbc_ref[0:tail, :] = extbc_ref[q:q + tail, :]

    row = lax.broadcasted_iota(jnp.int32, (q, q), 0)
    col = lax.broadcasted_iota(jnp.int32, (q, q), 1)
    lower = row >= col
    tri = lower.astype(BF16)
    upper = (row <= col).astype(BF16)

    small = small_ref[...]
    smallT = smallT_ref[...]
    neg_a = -jnp.exp(alog_ref[...])
    neg_aT = -jnp.exp(alogT_ref[...])
    dt = _softplus(small[:, 0:SSD_HEADS] + dtb_ref[...])
    dtT = _softplus(smallT[0:SSD_HEADS, :] + dtbT_ref[...])
    a_cum = _dot3_rhs(tri, dt * neg_a)
    a_cumT = _dot3_lhs(dtT * neg_aT, upper)
    a_last = a_cum[q - 1:q, :]

    logf = -_softplus(-(small[:, SSD_HEADS:SSD_HEADS + FOX_HEADS] + fb_ref[...]))
    cumf = fcarry_ref[...] + _dot3_rhs(tri, logf)
    cum_ref[...] = cumf
    fcarry_ref[...] = cumf[q - 1:q, :]

    expand = expand_ref[...]
    dt_full = _dot3_lhs(dt, expand)
    ea_full = _dot3_lhs(jnp.exp(a_cum), expand)
    ds_full = _dot3_lhs(jnp.exp(a_last - a_cum), expand)

    xdt = xs * dt_full
    xdt_b = xdt.astype(BF16)
    xw_b = (xdt * ds_full).astype(BF16)
    bc_b = bc.astype(BF16)
    lane = lax.broadcasted_iota(jnp.int32, (q, LANES), 1)

    pieces = []
    for g in range(SSD_GROUPS):
        b_g = bc_b[:, g * SSD_STATE:(g + 1) * SSD_STATE]
        c_g = bc_b[:, (SSD_GROUPS + g) * SSD_STATE:(SSD_GROUPS + g + 1) * SSD_STATE]
        cb = lax.dot_general(c_g, b_g, _NT, preferred_element_type=F32)
        heads_per_group = SSD_HEADS // SSD_GROUPS
        for pp in range(heads_per_group // 2):
            pair = g * (heads_per_group // 2) + pp
            xp = xdt_b[:, pair * LANES:(pair + 1) * LANES]
            ys = []
            for hh in range(2):
                h = 2 * pair + hh
                seg = a_cum[:, h:h + 1] - a_cumT[h:h + 1, :]
                decay = jnp.exp(jnp.where(lower, seg, -jnp.inf))
                m = (cb * decay).astype(BF16)
                ys.append(jnp.dot(m, xp, preferred_element_type=F32))
            pieces.append(jnp.where(lane < SSD_HEAD_DIM, ys[0], ys[1]))
    y_diag = jnp.concatenate(pieces, axis=1)

    offs = []
    for g in range(SSD_GROUPS):
        b_g = bc_b[:, g * SSD_STATE:(g + 1) * SSD_STATE]
        c_g = bc_b[:, (SSD_GROUPS + g) * SSD_STATE:(SSD_GROUPS + g + 1) * SSD_STATE]
        gs = slice(g * SSD_GROUP_WIDTH, (g + 1) * SSD_GROUP_WIDTH)
        st = state_ref[g]
        offs.append(jnp.dot(c_g, st.astype(BF16), preferred_element_type=F32))
        state_ref[g] = st * ea_full[q - 1:q, gs] + lax.dot_general(
            b_g, xw_b[:, gs], _TN, preferred_element_type=F32)
    y_off = jnp.concatenate(offs, axis=1) * ea_full

    y = y_diag + y_off + dfull_ref[...] * xs
    y = y * _silu(z_ref[...].astype(F32))
    normed = []
    for g in range(SSD_GROUPS):
        yg = y[:, g * SSD_GROUP_WIDTH:(g + 1) * SSD_GROUP_WIDTH]
        normed.append(yg * lax.rsqrt(jnp.mean(yg * yg, axis=-1, keepdims=True) + EPS))
    y_ref[...] = (jnp.concatenate(normed, axis=1) * normw_ref[...]).astype(BF16)


def _ssd(main, small, smallT, conv_w, conv_b, dt_bias, a_log, d_skip, ssd_norm_w, fox_f_bias, *, q):
    L = main.shape[0]
    expand = (jnp.arange(SSD_INNER)[None, :] // SSD_HEAD_DIM == jnp.arange(SSD_HEADS)[:, None]).astype(BF16)
    const = lambda shape: pl.BlockSpec(shape, lambda c: (0,) * len(shape))
    return pl.pallas_call(
        functools.partial(_ssd_kernel, q=q),
        grid=(L // q,),
        in_specs=[
            pl.BlockSpec((q, SSD_INNER), lambda c: (c, COL_Z // SSD_INNER)),
            pl.BlockSpec((q, SSD_INNER), lambda c: (c, COL_XS // SSD_INNER)),
            pl.BlockSpec((q, SSD_BC), lambda c: (c, COL_BC // SSD_BC)),
            pl.BlockSpec((q, SMALL_DIM), lambda c: (c, 0)),
            pl.BlockSpec((SMALL_DIM, q), lambda c: (0, c)),
            const((SSD_CONV, SSD_INNER)), const((1, SSD_INNER)),
            const((SSD_CONV, SSD_BC)), const((1, SSD_BC)),
            const((1, SSD_HEADS)), const((SSD_HEADS, 1)),
            const((1, SSD_HEADS)), const((SSD_HEADS, 1)),
            const((1, FOX_HEADS)),
            const((1, SSD_INNER)), const((1, SSD_INNER)),
            const((SSD_HEADS, SSD_INNER)),
        ],
        out_specs=[
            pl.BlockSpec((q, SSD_INNER), lambda c: (c, 0)),
            pl.BlockSpec((q, FOX_HEADS), lambda c: (c, 0)),
        ],
        out_shape=[
            jax.ShapeDtypeStruct((L, SSD_INNER), BF16),
            jax.ShapeDtypeStruct((L, FOX_HEADS), F32),
        ],
        scratch_shapes=[
            pltpu.VMEM((q + 8, SSD_INNER), F32),
            pltpu.VMEM((q + 8, SSD_BC), F32),
            pltpu.VMEM((SSD_GROUPS, SSD_STATE, SSD_GROUP_WIDTH), F32),
            pltpu.VMEM((1, FOX_HEADS), F32),
        ],
        compiler_params=pltpu.CompilerParams(
            dimension_semantics=("arbitrary",), vmem_limit_bytes=VMEM_LIMIT),
        name="ssd",
    )(main, main, main, small, smallT,
      conv_w[:, :SSD_INNER], conv_b[:SSD_INNER].reshape(1, -1),
      conv_w[:, SSD_INNER:], conv_b[SSD_INNER:].reshape(1, -1),
      dt_bias.reshape(1, -1), dt_bias.reshape(-1, 1),
      a_log.reshape(1, -1), a_log.reshape(-1, 1),
      fox_f_bias.reshape(1, -1),
      jnp.repeat(d_skip, SSD_HEAD_DIM).reshape(1, -1), ssd_norm_w.reshape(1, -1),
      expand)


FOX_AUG = 3
FOX_SKIP_LOG2 = 160.0


def _foxprep_kernel(q_ref, k_ref, v_ref, cum_ref, kaug_ref, qaugT_ref, vaugT_ref, norm_ref):
    p = pl.program_id(1)
    stat_row = lax.broadcasted_iota(jnp.int32, (8, LANES), 0)
    stats = jnp.zeros((8, LANES), F32)
    tb = q_ref.shape[0]
    lane = lax.broadcasted_iota(jnp.int32, (tb, LANES), 1)
    cum = cum_ref[...]
    rel = (cum[0:1, :] - cum) * LOG2E
    pieces = _split3(rel)
    hrow = lax.broadcasted_iota(jnp.int32, (FOX_HEADS, LANES), 0)
    hlane = lax.broadcasted_iota(jnp.int32, (FOX_HEADS, LANES), 1)
    q = q_ref[...].astype(F32) * (FOX_HEAD_DIM ** -0.5 * LOG2E)
    k = k_ref[...]
    v = v_ref[...].astype(F32)
    eye = jnp.where(lax.broadcasted_iota(jnp.int32, (LANES, LANES), 0)
                    == lax.broadcasted_iota(jnp.int32, (LANES, LANES), 1), 1.0, 0.0).astype(BF16)
    transpose = lambda xb: lax.dot_general(eye, xb, _NT, preferred_element_type=F32)
    trow = lax.broadcasted_iota(jnp.int32, (LANES, tb), 0)
    kT = transpose(k)
    for hh in range(2):
        head_lanes = (lane < FOX_HEAD_DIM) if hh == 0 else (lane >= FOX_HEAD_DIM)
        base = FOX_HEAD_DIM if hh == 0 else 0
        h = 2 * p + hh
        aug = None
        for r in range(FOX_AUG):
            place = jnp.where(hrow == h, jnp.where(hlane == base + r, 1.0, 0.0), 0.0).astype(BF16)
            term = jnp.dot(pieces[r], place, preferred_element_type=F32)
            aug = term if aug is None else aug + term
        kaug_ref[hh] = jnp.where(head_lanes, k, aug.astype(BF16))
        is_aug = (lane >= base) & (lane < base + FOX_AUG)
        qa = jnp.where(head_lanes, q, jnp.where(is_aug, 1.0, 0.0)).astype(BF16)
        va = jnp.where(head_lanes, v, jnp.where(lane == base, 1.0, 0.0)).astype(BF16)
        qaT = transpose(qa)
        qaugT_ref[hh] = qaT.astype(BF16)
        vaugT_ref[hh] = transpose(va).astype(BF16)
        head_rows = (trow < FOX_HEAD_DIM) if hh == 0 else (trow >= FOX_HEAD_DIM)
        sq_norms = lambda xT: jnp.sum(jnp.where(head_rows, xT * xT, 0.0), axis=0, keepdims=True)
        qn2 = jnp.max(sq_norms(qaT), axis=1, keepdims=True)
        kn2 = jnp.max(sq_norms(kT), axis=1, keepdims=True)
        stats = jnp.where(stat_row == hh, qn2, jnp.where(stat_row == 2 + hh, kn2, stats))
    norm_ref[...] = stats


def _foxprep(main, cum, *, tb):
    L = main.shape[0]
    pairs = FOX_HEADS // 2
    return pl.pallas_call(
        _foxprep_kernel,
        grid=(L // tb, pairs),
        in_specs=[
            pl.BlockSpec((tb, LANES), lambda b, p: (b, COL_Q // LANES + p)),
            pl.BlockSpec((tb, LANES), lambda b, p: (b, COL_K // LANES + p)),
            pl.BlockSpec((tb, LANES), lambda b, p: (b, COL_V // LANES + p)),
            pl.BlockSpec((tb, FOX_HEADS), lambda b, p: (b, 0)),
        ],
        out_specs=[
            pl.BlockSpec((2, tb, LANES), lambda b, p: (p, b, 0)),
            pl.BlockSpec((2, LANES, tb), lambda b, p: (p, 0, b)),
            pl.BlockSpec((2, LANES, tb), lambda b, p: (p, 0, b)),
            pl.BlockSpec((None, None, 8, LANES), lambda b, p: (b, p, 0, 0)),
        ],
        out_shape=[
            jax.ShapeDtypeStruct((FOX_HEADS, L, LANES), BF16),
            jax.ShapeDtypeStruct((FOX_HEADS, LANES, L), BF16),
            jax.ShapeDtypeStruct((FOX_HEADS, LANES, L), BF16),
            jax.ShapeDtypeStruct((L // tb, pairs, 8, LANES), F32),
        ],
        compiler_params=pltpu.CompilerParams(
            dimension_semantics=("parallel", "parallel"), vmem_limit_bytes=VMEM_LIMIT),
        name="foxprep",
    )(main, main, main, cum)


def _fox_kernel(tp_ref, ti_ref, tj_ref, first_ref, count_ref, cq_ref, ck_ref,
                kaug_ref, qaugT_ref, vaugT_ref, o_ref, m_ref, acc_ref, *, tq, tk):
    n = pl.program_id(0)
    p = tp_ref[n]
    i = ti_ref[n]
    j = tj_ref[n]
    live = n < count_ref[0]
    last = ((i + 1) * tq - 1) // tk

    @pl.when(jnp.logical_and(live, first_ref[n] == 1))
    def _():
        m_ref[...] = jnp.full(m_ref.shape, NEG_BIG, F32)
        acc_ref[...] = jnp.zeros(acc_ref.shape, F32)

    def step(masked):
        if masked:
            key = j * tk + lax.broadcasted_iota(jnp.int32, (tk, tq), 0)
            qry = i * tq + lax.broadcasted_iota(jnp.int32, (tk, tq), 1)
            valid = key <= qry
        scores = [jnp.dot(kaug_ref[hh], qaugT_ref[hh], preferred_element_type=F32) for hh in range(2)]
        for hh in range(2):
            h = 2 * p + hh
            d = cq_ref[i * FOX_HEADS + h] - ck_ref[j * FOX_HEADS + h]
            sT = scores[hh]
            if masked:
                sT = jnp.where(valid, sT, NEG_BIG)
            m_prev = m_ref[hh]
            m_new = jnp.maximum(m_prev, jnp.max(sT, axis=0, keepdims=True) + d)
            alpha = jnp.exp2(m_prev - m_new)
            pT = jnp.exp2((sT - (m_new - d)).astype(BF16))
            acc_ref[hh] = alpha * acc_ref[hh] + jnp.dot(vaugT_ref[hh], pT, preferred_element_type=F32)
            m_ref[hh] = m_new

    crosses_diagonal = (j + 1) * tk - 1 > i * tq
    pl.when(jnp.logical_and(live, crosses_diagonal))(lambda: step(True))
    pl.when(jnp.logical_and(live, jnp.logical_not(crosses_diagonal)))(lambda: step(False))

    @pl.when(jnp.logical_and(live, j == last))
    def _():
        a0 = acc_ref[0]
        a1 = acc_ref[1]
        o0 = a0[0:FOX_HEAD_DIM, :] / a0[FOX_HEAD_DIM:FOX_HEAD_DIM + 1, :]
        o1 = a1[FOX_HEAD_DIM:, :] / a1[0:1, :]
        o_ref[...] = jnp.concatenate([o0, o1], axis=0).T.astype(BF16)


def _fox_tables(cum, norms, *, tq, tk):
    L = cum.shape[0]
    nq, nk = L // tq, L // tk
    pairs = FOX_HEADS // 2
    assert nq < 256 and nk < 256
    pp, ii, jj = np.meshgrid(np.arange(pairs), np.arange(nq), np.arange(nk), indexing="ij")
    causal = jj * tk <= (ii + 1) * tq - 1
    diagonal = (jj + 1) * tk - 1 > ii * tq
    code = jnp.asarray((pp << 16 | ii << 8 | jj).reshape(-1), jnp.int32)
    n_steps = int(causal.sum())

    qn = jnp.sqrt(norms[:, :, 0:2, 0]).reshape(nk, FOX_HEADS)
    kn = jnp.sqrt(norms[:, :, 2:4, 0]).reshape(nk, FOX_HEADS)
    qn_i = qn.reshape(nq, tq // tk, FOX_HEADS).max(axis=1)
    kd_i = kn.reshape(nq, tq // tk, FOX_HEADS).max(axis=1)
    cq = cum[::tq] * LOG2E
    ck = cum[::tk] * LOG2E
    c_last = cum[tk - 1::tk] * LOG2E
    bound = qn_i[:, None, :] * (kn[None, :, :] + kd_i[:, None, :]) + (cq[:, None, :] - c_last[None, :, :])
    alive = (bound >= -FOX_SKIP_LOG2).reshape(nq, nk, pairs, 2).any(axis=-1)
    live = (jnp.asarray(causal) & (jnp.asarray(diagonal) | jnp.transpose(alive, (2, 0, 1)))).reshape(-1)
    pos = jnp.cumsum(live.astype(jnp.int32)) - 1
    count = pos[-1] + 1
    slot = jnp.arange(n_steps, dtype=jnp.int32)
    hit = live[None, :] & (pos[None, :] == slot[:, None])
    picked = jnp.sum(jnp.where(hit, code[None, :], 0), axis=1)
    picked = jnp.where(slot < count, picked, jnp.max(jnp.where(live, code, -1)))
    tp, ti, tj = picked >> 16, (picked >> 8) & 255, picked & 255
    first = jnp.concatenate([jnp.ones((1,), jnp.int32),
                             ((tp[1:] != tp[:-1]) | (ti[1:] != ti[:-1])).astype(jnp.int32)])
    return tp, ti, tj, first, count.reshape(1), cq.reshape(-1), ck.reshape(-1)


def _fox(main, cum, *, tq, tk):
    L = main.shape[0]
    kaug, qaugT, vaugT, norms = _foxprep(main, cum, tb=tk)
    tables = _fox_tables(cum, norms, tq=tq, tk=tk)
    grid_spec = pltpu.PrefetchScalarGridSpec(
        num_scalar_prefetch=len(tables),
        grid=(tables[0].shape[0],),
        in_specs=[
            pl.BlockSpec((2, tk, LANES), lambda n, tp, ti, tj, *_: (tp[n], tj[n], 0)),
            pl.BlockSpec((2, LANES, tq), lambda n, tp, ti, tj, *_: (tp[n], 0, ti[n])),
            pl.BlockSpec((2, LANES, tk), lambda n, tp, ti, tj, *_: (tp[n], 0, tj[n])),
        ],
        out_specs=pl.BlockSpec((tq, LANES), lambda n, tp, ti, tj, *_: (ti[n], tp[n])),
        scratch_shapes=[
            pltpu.VMEM((2, 1, tq), F32),
            pltpu.VMEM((2, LANES, tq), F32),
        ],
    )
    return pl.pallas_call(
        functools.partial(_fox_kernel, tq=tq, tk=tk),
        grid_spec=grid_spec,
        out_shape=jax.ShapeDtypeStruct((L, FOX_INNER), BF16),
        compiler_params=pltpu.CompilerParams(
            dimension_semantics=("arbitrary",), vmem_limit_bytes=VMEM_LIMIT),
        name="fox",
    )(*tables, kaug, qaugT, vaugT)


def _mid_kernel(x_ref, ys_ref, yf_ref, wo_ref, ln2_ref, wq_ref, k1_ref, k2_ref,
                x1_ref, h2T_ref, s1T_ref, s2T_ref):
    mix = jnp.dot(ys_ref[...], wo_ref[0:SSD_INNER, :], preferred_element_type=F32)
    mix = mix + jnp.dot(yf_ref[...], wo_ref[SSD_INNER:, :], preferred_element_type=F32)
    x1 = x_ref[...] + mix
    x1_ref[...] = x1
    h2 = x1 * lax.rsqrt(jnp.mean(x1 * x1, axis=-1, keepdims=True) + EPS) * ln2_ref[...]
    h2T_ref[...] = h2.T.astype(BF16)
    qp = jnp.dot(h2.astype(BF16), wq_ref[...], preferred_element_type=F32)
    half = PEER_KEY_DIM // 2
    for h in range(PEER_HEADS):
        q1 = qp[:, h * PEER_KEY_DIM:h * PEER_KEY_DIM + half].astype(BF16)
        q2 = qp[:, h * PEER_KEY_DIM + half:(h + 1) * PEER_KEY_DIM].astype(BF16)
        s1T_ref[h] = lax.dot_general(k1_ref[h], q1, _NT, preferred_element_type=F32)
        s2T_ref[h] = lax.dot_general(k2_ref[h], q2, _NT, preferred_element_type=F32)


def _mid(x2d, y_ssd, y_fox, w_out, ln2_w, wq, k1, k2, *, tm):
    L = x2d.shape[0]
    half = PEER_KEY_DIM // 2
    resident = lambda shape: pl.BlockSpec(shape, lambda i: (0,) * len(shape), pipeline_mode=pl.Buffered(1))
    return pl.pallas_call(
        _mid_kernel,
        grid=(L // tm,),
        in_specs=[
            pl.BlockSpec((tm, D_MODEL), lambda i: (i, 0)),
            pl.BlockSpec((tm, SSD_INNER), lambda i: (i, 0)),
            pl.BlockSpec((tm, FOX_INNER), lambda i: (i, 0)),
            resident((D_MODEL, D_MODEL)),
            resident((1, D_MODEL)),
            resident((D_MODEL, PEER_HEADS * PEER_KEY_DIM)),
            resident((PEER_HEADS, PEER_N_KEYS, half)),
            resident((PEER_HEADS, PEER_N_KEYS, half)),
        ],
        out_specs=[
            pl.BlockSpec((tm, D_MODEL), lambda i: (i, 0)),
            pl.BlockSpec((D_MODEL, tm), lambda i: (0, i)),
            pl.BlockSpec((PEER_HEADS, PEER_N_KEYS, tm), lambda i: (0, 0, i)),
            pl.BlockSpec((PEER_HEADS, PEER_N_KEYS, tm), lambda i: (0, 0, i)),
        ],
        out_shape=[
            jax.ShapeDtypeStruct((L, D_MODEL), F32),
            jax.ShapeDtypeStruct((D_MODEL, L), BF16),
            jax.ShapeDtypeStruct((PEER_HEADS, PEER_N_KEYS, L), F32),
            jax.ShapeDtypeStruct((PEER_HEADS, PEER_N_KEYS, L), F32),
        ],
        compiler_params=pltpu.CompilerParams(
            dimension_semantics=("parallel",), vmem_limit_bytes=VMEM_LIMIT),
        name="mid",
    )(x2d, y_ssd, y_fox, w_out, ln2_w.reshape(1, D_MODEL), wq, k1, k2)


def _top16(s, order, exact):
    axis = s.ndim - 2
    slot = lax.broadcasted_iota(jnp.int32, s.shape[:-2] + (PEER_TOPK, s.shape[-1]), axis)
    unranked = float(PEER_TOPK)

    if not exact:
        def next_value(r, carry):
            below, vals = carry
            m = jnp.max(jnp.where(s < below, s, -jnp.inf), axis=axis, keepdims=True)
            return m, jnp.where(slot == r, m, vals)

        ceiling = jnp.full(s.shape[:-2] + (1, s.shape[-1]), jnp.inf, F32)
        lowest, vals = lax.fori_loop(0, PEER_TOPK, next_value, (ceiling, jnp.zeros(slot.shape, F32)))
        taken = s >= lowest
        return None, vals, taken, jnp.sum(jnp.where(taken, 1.0, 0.0), axis=axis, keepdims=True)

    def extract(r, carry):
        s, rank, vals = carry
        m = jnp.max(s, axis=axis, keepdims=True)
        first = jnp.min(jnp.where(s == m, order, jnp.inf), axis=axis, keepdims=True)
        hit = order == first
        rank = jnp.where(hit, lax.convert_element_type(r, F32), rank)
        s = jnp.where(hit, -jnp.inf, s)
        return s, rank, jnp.where(slot == r, m, vals)

    init = (s, jnp.full(s.shape, unranked, F32), jnp.zeros(slot.shape, F32))
    _, rank, vals = lax.fori_loop(0, PEER_TOPK, extract, init)
    taken = rank < unranked
    return rank, vals, taken, jnp.sum(jnp.where(taken, 1.0, 0.0), axis=axis, keepdims=True)


def _topk_kernel(s1T_ref, s2T_ref, lim_ref, e1_ref, r2_ref, e2_ref):
    t = s1T_ref.shape[2]
    key_id = lax.broadcasted_iota(jnp.int32, (PEER_N_KEYS, t), 0).astype(F32)
    half_k = PEER_TOPK // 2
    pos_main = lax.broadcasted_iota(jnp.int32, (half_k * PEER_TOPK, t), 0)
    pos_tail = (lax.broadcasted_iota(jnp.int32, (half_k, t), 0) + half_k) * PEER_TOPK
    pos = jnp.concatenate([pos_main, pos_tail], axis=0).astype(F32)
    slot = lax.broadcasted_iota(jnp.int32, (PEER_TOPK, t), 0)

    def solve(h, exact):
        s1 = s1T_ref[h]
        s2 = s2T_ref[h]
        rank12, v12, _, n12 = _top16(jnp.stack([s1, s2]), key_id, exact)
        v1, v2 = v12[0], v12[1]

        cand = jnp.concatenate(
            [v1[a:a + 1, :] + v2 for a in range(half_k)] + [v1[half_k:, :] + v2[0:1, :]], axis=0)
        _, top_sums, taken_c, n_c = _top16(cand, pos, exact)
        sel = jnp.where(taken_c, 1.0, 0.0)
        z = jnp.sum(jnp.exp(top_sums - top_sums[0:1, :]), axis=0, keepdims=True)
        n_taken = jnp.maximum(jnp.maximum(n12[0], n12[1]), n_c)

        if exact:
            rank1, rank2 = rank12[0], rank12[1]
            is_rank1 = lambda a: rank1 == float(a)
        else:
            rank2 = jnp.full(s2.shape, float(PEER_TOPK), F32)
            for a in range(PEER_TOPK):
                rank2 = jnp.where(s2 == v2[a:a + 1, :], float(a), rank2)
            is_rank1 = lambda a: s1 == v1[a:a + 1, :]

        bcount = jnp.zeros((PEER_TOPK, t), F32)
        for a in range(half_k):
            cnt = jnp.sum(sel[a * PEER_TOPK:(a + 1) * PEER_TOPK, :], axis=0, keepdims=True)
            bcount = jnp.where(slot == a, cnt, bcount)
        tail = jnp.concatenate([jnp.zeros((half_k, t), F32), sel[half_k * PEER_TOPK:, :]], axis=0)
        bcount = jnp.where(slot >= half_k, tail, bcount)

        lim = jnp.zeros((PEER_N_KEYS, t), F32)
        for a in range(PEER_TOPK):
            lim = jnp.where(is_rank1(a), bcount[a:a + 1, :], lim)

        lim_ref[h] = lim
        e1_ref[h] = jnp.exp(s1 - v1[0:1, :])
        r2_ref[h] = rank2.astype(BF16)
        e2_ref[h] = (jnp.exp(s2 - v2[0:1, :]) / z).astype(BF16)
        return jnp.max(n_taken)

    def head(h, _):
        most = solve(h, exact=False)

        @pl.when(most > float(PEER_TOPK))
        def _():
            solve(h, exact=True)

        return 0

    lax.fori_loop(0, PEER_HEADS, head, 0)


def _topk(s1T, s2T, *, tt):
    L = s1T.shape[2]
    spec = pl.BlockSpec((PEER_HEADS, PEER_N_KEYS, tt), lambda i: (0, 0, i))
    shape = jax.ShapeDtypeStruct((PEER_HEADS, PEER_N_KEYS, L), F32)
    shape_b = jax.ShapeDtypeStruct((PEER_HEADS, PEER_N_KEYS, L), BF16)
    return pl.pallas_call(
        _topk_kernel,
        grid=(L // tt,),
        in_specs=[spec, spec],
        out_specs=[spec] * 4,
        out_shape=[shape, shape, shape_b, shape_b],
        compiler_params=pltpu.CompilerParams(
            dimension_semantics=("parallel",), vmem_limit_bytes=VMEM_LIMIT),
        name="topk",
    )(s1T, s2T)


def _peer_kernel(h2T_ref, u_ref, unext_ref, vT_ref, lim_ref, e1_ref, r2_ref, e2_ref, x1_ref, lnf_ref,
                 o_ref, acc_ref, act0_ref, *, et, sub):
    e = pl.program_id(1)
    n_sub = et // sub
    keys_per_sub = sub // PEER_N_KEYS

    def activation(rows_ref, s):
        return jnp.dot(rows_ref[s * sub:(s + 1) * sub, :], h2T_ref[...], preferred_element_type=F32)

    @pl.when(e == 0)
    def _():
        acc_ref[...] = jnp.zeros(acc_ref.shape, F32)
        act0_ref[...] = activation(u_ref, 0)

    def gate(s):
        parts = []
        for i in range(keys_per_sub):
            i1 = e * (et // PEER_N_KEYS) + s * keys_per_sub + i
            w = None
            for h in range(PEER_HEADS):
                lim_row = lim_ref[h, pl.ds(i1, 1), :].astype(BF16)
                e1_row = e1_ref[h, pl.ds(i1, 1), :].astype(BF16)
                term = jnp.where(r2_ref[h] < lim_row, e2_ref[h] * e1_row, jnp.zeros((), BF16))
                w = term if w is None else w + term
            parts.append(w)
        return jnp.concatenate(parts, axis=0)

    total = None
    a = act0_ref[...]
    for s in range(n_sub):
        a_next = activation(u_ref, s + 1) if s + 1 < n_sub else activation(unext_ref, 0)
        gelu = 0.5 * a * (1.0 + lax.erf(a * (2.0 ** -0.5)))
        wg = gelu.astype(BF16) * gate(s)
        part = jnp.dot(vT_ref[:, s * sub:(s + 1) * sub], wg, preferred_element_type=F32)
        total = part if total is None else total + part
        a = a_next
    act0_ref[...] = a
    acc_ref[...] += total

    @pl.when(e == pl.num_programs(1) - 1)
    def _():
        x2 = x1_ref[...] + acc_ref[...].T
        o_ref[...] = x2 * lax.rsqrt(jnp.mean(x2 * x2, axis=-1, keepdims=True) + EPS) * lnf_ref[...]


def _peer(h2T, u_b, vT_b, lim, e1, r2, e2, x1, lnf_w, *, tt, et, sub):
    L = x1.shape[0]
    tok = pl.BlockSpec((PEER_HEADS, PEER_N_KEYS, tt), lambda t, e: (0, 0, t), pipeline_mode=pl.Buffered(1))
    n_blocks = PEER_N_EXPERTS // et
    return pl.pallas_call(
        functools.partial(_peer_kernel, et=et, sub=sub),
        grid=(L // tt, n_blocks),
        in_specs=[
            pl.BlockSpec((D_MODEL, tt), lambda t, e: (0, t), pipeline_mode=pl.Buffered(1)),
            pl.BlockSpec((et, D_MODEL), lambda t, e: (e, 0)),
            pl.BlockSpec((sub, D_MODEL), lambda t, e: (jnp.minimum(e + 1, n_blocks - 1) * (et // sub), 0)),
            pl.BlockSpec((D_MODEL, et), lambda t, e: (0, e)),
            tok, tok, tok, tok,
            pl.BlockSpec((tt, D_MODEL), lambda t, e: (t, 0), pipeline_mode=pl.Buffered(1)),
            pl.BlockSpec((1, D_MODEL), lambda t, e: (0, 0)),
        ],
        out_specs=pl.BlockSpec((tt, D_MODEL), lambda t, e: (t, 0)),
        out_shape=jax.ShapeDtypeStruct((L, D_MODEL), F32),
        scratch_shapes=[
            pltpu.VMEM((D_MODEL, tt), F32),
            pltpu.VMEM((sub, tt), F32),
        ],
        compiler_params=pltpu.CompilerParams(
            dimension_semantics=("parallel", "arbitrary"), vmem_limit_bytes=VMEM_LIMIT),
        name="peer",
    )(h2T, u_b, u_b, vT_b, lim, e1, r2, e2, x1, lnf_w.reshape(1, D_MODEL))


def _tiles(L):
    return dict(
        inproj_tm=min(512, L), inproj_tn=2816,
        ssd_q=min(128, L),
        fox_tq=min(1024, L), fox_tk=min(512, L),
        mid_tm=min(256, L),
        topk_tt=min(256, L),
        peer_tt=min(512, L), peer_et=1024, peer_sub=256,
    )


def kernel(x, ln1_w, w_in, conv_w, conv_b, dt_bias, a_log, d_skip, ssd_norm_w, fox_f_bias,
           w_out, ln2_w, peer_wq, peer_k1, peer_k2, peer_u, peer_v, lnf_w):
    B, L, D = x.shape
    assert B == 1 and D == D_MODEL
    t = _tiles(L)
    x2d = x.reshape(L, D)

    c_z, c_xbc, c_dt, c_q = 0, SSD_INNER, 2560, 2576
    c_f = c_q + 3 * FOX_INNER
    w_main = jnp.concatenate([w_in[:, c_z:c_dt], w_in[:, c_q:c_f]], axis=1).astype(BF16)
    w_small = jnp.concatenate(
        [w_in[:, c_dt:c_q], w_in[:, c_f:], jnp.zeros((D, SMALL_DIM - SSD_HEADS - FOX_HEADS), F32)],
        axis=1).astype(BF16)

    main, small, smallT = _inproj(x2d, ln1_w, w_main, w_small, tm=t["inproj_tm"], tn=t["inproj_tn"])
    y_ssd, cum = _ssd(main, small, smallT, conv_w, conv_b, dt_bias, a_log, d_skip, ssd_norm_w,
                      fox_f_bias, q=t["ssd_q"])
    y_fox = _fox(main, cum, tq=t["fox_tq"], tk=t["fox_tk"])
    x1, h2T, s1T, s2T = _mid(x2d, y_ssd, y_fox, w_out.astype(BF16), ln2_w, peer_wq.astype(BF16),
                             peer_k1.astype(BF16), peer_k2.astype(BF16), tm=t["mid_tm"])
    lim, e1, r2, e2 = _topk(s1T, s2T, tt=t["topk_tt"])
    out = _peer(h2T, peer_u.astype(BF16), peer_v.T.astype(BF16), lim, e1, r2, e2, x1, lnf_w,
                tt=t["peer_tt"], et=t["peer_et"], sub=t["peer_sub"])
    return out.reshape(B, L, D)
```

```python
import functools

import numpy as np
import jax
import jax.numpy as jnp
from jax import lax
from jax.experimental import pallas as pl
from jax.experimental.pallas import tpu as pltpu

F32 = jnp.float32
BF16 = jnp.bfloat16

D_MODEL = 2048
EPS = 1e-6

SSD_HEAD_DIM = 64
SSD_INNER = 1024
SSD_HEADS = 16
SSD_GROUPS = 2
SSD_STATE = 128
SSD_CONV = 4
SSD_BC = 2 * SSD_GROUPS * SSD_STATE
SSD_GROUP_WIDTH = SSD_INNER // SSD_GROUPS

FOX_HEAD_DIM = 64
FOX_INNER = 1024
FOX_HEADS = 16

PEER_HEADS = 8
PEER_N_KEYS = 128
PEER_N_EXPERTS = PEER_N_KEYS * PEER_N_KEYS
PEER_KEY_DIM = 256
PEER_TOPK = 16

COL_Q = 0
COL_K = 1024
COL_V = 2048
COL_Z = 3072
COL_XS = 4096
COL_BC = 5120
MAIN_DIM = 5632
SMALL_DIM = 128

LANES = 128
VMEM_LIMIT = 56 * 1024 * 1024

NEG_BIG = -1e30
LOG2E = 1.4426950408889634

_NT = (((1,), (1,)), ((), ()))
_TN = (((0,), (0,)), ((), ()))


def _softplus(x):
    return jnp.maximum(x, 0.0) + jnp.log1p(jnp.exp(-jnp.abs(x)))


def _silu(x):
    return x * jax.nn.sigmoid(x)


def _split3(x):
    hi = x.astype(BF16)
    r = x - hi.astype(F32)
    mid = r.astype(BF16)
    lo = (r - mid.astype(F32)).astype(BF16)
    return hi, mid, lo


def _dot3_lhs(x, w):
    a, b, c = _split3(x)
    d = lambda p: jnp.dot(p, w, preferred_element_type=F32)
    return d(a) + d(b) + d(c)


def _dot3_rhs(w, x):
    a, b, c = _split3(x)
    d = lambda p: jnp.dot(w, p, preferred_element_type=F32)
    return d(a) + d(b) + d(c)


def _inproj_kernel(x_ref, lnw_ref, wm_ref, ws_ref, wsT_ref, main_ref, small_ref, smallT_ref, h_ref):
    @pl.when(pl.program_id(1) == 0)
    def _():
        x = x_ref[...]
        h = x * lax.rsqrt(jnp.mean(x * x, axis=-1, keepdims=True) + EPS) * lnw_ref[...]
        hb = h.astype(BF16)
        h_ref[...] = hb
        small_ref[...] = jnp.dot(hb, ws_ref[...], preferred_element_type=F32)
        smallT_ref[...] = lax.dot_general(wsT_ref[...], hb, _NT, preferred_element_type=F32)

    main_ref[...] = jnp.dot(h_ref[...], wm_ref[...], preferred_element_type=F32).astype(BF16)


def _inproj(x2d, ln1_w, w_main, w_small, *, tm, tn):
    L = x2d.shape[0]
    return pl.pallas_call(
        _inproj_kernel,
        grid=(L // tm, MAIN_DIM // tn),
        in_specs=[
            pl.BlockSpec((tm, D_MODEL), lambda i, j: (i, 0)),
            pl.BlockSpec((1, D_MODEL), lambda i, j: (0, 0)),
            pl.BlockSpec((D_MODEL, tn), lambda i, j: (0, j)),
            pl.BlockSpec((D_MODEL, SMALL_DIM), lambda i, j: (0, 0)),
            pl.BlockSpec((SMALL_DIM, D_MODEL), lambda i, j: (0, 0)),
        ],
        out_specs=[
            pl.BlockSpec((tm, tn), lambda i, j: (i, j)),
            pl.BlockSpec((tm, SMALL_DIM), lambda i, j: (i, 0)),
            pl.BlockSpec((SMALL_DIM, tm), lambda i, j: (0, i)),
        ],
        out_shape=[
            jax.ShapeDtypeStruct((L, MAIN_DIM), BF16),
            jax.ShapeDtypeStruct((L, SMALL_DIM), F32),
            jax.ShapeDtypeStruct((SMALL_DIM, L), F32),
        ],
        scratch_shapes=[pltpu.VMEM((tm, D_MODEL), BF16)],
        compiler_params=pltpu.CompilerParams(
            dimension_semantics=("parallel", "arbitrary"), vmem_limit_bytes=VMEM_LIMIT),
        name="inproj",
    )(x2d, ln1_w.reshape(1, D_MODEL), w_main, w_small, w_small.T)


def _ssd_kernel(z_ref, xs_ref, bc_ref, small_ref, smallT_ref,
                cwx_ref, cbx_ref, cwbc_ref, cbbc_ref,
                dtb_ref, dtbT_ref, alog_ref, alogT_ref, fb_ref,
                dfull_ref, normw_ref, expand_ref,
                y_ref, cum_ref,
                extx_ref, extbc_ref, state_ref, fcarry_ref, *, q):
    c = pl.program_id(0)
    tail = 8

    @pl.when(c == 0)
    def _():
        extx_ref[0:tail, :] = jnp.zeros((tail, SSD_INNER), F32)
        extbc_ref[0:tail, :] = jnp.zeros((tail, SSD_BC), F32)
        state_ref[...] = jnp.zeros(state_ref.shape, F32)
        fcarry_ref[...] = jnp.zeros(fcarry_ref.shape, F32)

    extx_ref[tail:tail + q, :] = xs_ref[...].astype(F32)
    extbc_ref[tail:tail + q, :] = bc_ref[...].astype(F32)

    def conv(ext_ref, w_ref, b_ref):
        acc = b_ref[...]
        for k in range(SSD_CONV):
            off = tail - (SSD_CONV - 1) + k
            acc = acc + ext_ref[off:off + q, :] * w_ref[k:k + 1, :]
        return acc

    xs = _silu(conv(extx_ref, cwx_ref, cbx_ref))
    bc = _silu(conv(extbc_ref, cwbc_ref, cbbc_ref))
    extx_ref[0:tail, :] = extx_ref[q:q + tail, :]
    extbc_ref[0:tail, :] = extbc_ref[q:q + tail, :]

    row = lax.broadcasted_iota(jnp.int32, (q, q), 0)
    col = lax.broadcasted_iota(jnp.int32, (q, q), 1)
    lower = row >= col
    tri = lower.astype(BF16)
    upper = (row <= col).astype(BF16)

    small = small_ref[...]
    smallT = smallT_ref[...]
    neg_a = -jnp.exp(alog_ref[...])
    neg_aT = -jnp.exp(alogT_ref[...])
    dt = _softplus(small[:, 0:SSD_HEADS] + dtb_ref[...])
    dtT = _softplus(smallT[0:SSD_HEADS, :] + dtbT_ref[...])
    a_cum = _dot3_rhs(tri, dt * neg_a)
    a_cumT = _dot3_lhs(dtT * neg_aT, upper)
    a_last = a_cum[q - 1:q, :]

    logf = -_softplus(-(small[:, SSD_HEADS:SSD_HEADS + FOX_HEADS] + fb_ref[...]))
    cumf = fcarry_ref[...] + _dot3_rhs(tri, logf)
    cum_ref[...] = cumf
    fcarry_ref[...] = cumf[q - 1:q, :]

    expand = expand_ref[...]
    dt_full = _dot3_lhs(dt, expand)
    ea_full = _dot3_lhs(jnp.exp(a_cum), expand)
    ds_full = _dot3_lhs(jnp.exp(a_last - a_cum), expand)

    xdt = xs * dt_full
    xdt_b = xdt.astype(BF16)
    xw_b = (xdt * ds_full).astype(BF16)
    bc_b = bc.astype(BF16)
    lane = lax.broadcasted_iota(jnp.int32, (q, LANES), 1)

    pieces = []
    for g in range(SSD_GROUPS):
        b_g = bc_b[:, g * SSD_STATE:(g + 1) * SSD_STATE]
        c_g = bc_b[:, (SSD_GROUPS + g) * SSD_STATE:(SSD_GROUPS + g + 1) * SSD_STATE]
        cb = lax.dot_general(c_g, b_g, _NT, preferred_element_type=F32)
        heads_per_group = SSD_HEADS // SSD_GROUPS
        for pp in range(heads_per_group // 2):
            pair = g * (heads_per_group // 2) + pp
            xp = xdt_b[:, pair * LANES:(pair + 1) * LANES]
            ys = []
            for hh in range(2):
                h = 2 * pair + hh
                seg = a_cum[:, h:h + 1] - a_cumT[h:h + 1, :]
                decay = jnp.exp(jnp.where(lower, seg, -jnp.inf))
                m = (cb * decay).astype(BF16)
                ys.append(jnp.dot(m, xp, preferred_element_type=F32))
            pieces.append(jnp.where(lane < SSD_HEAD_DIM, ys[0], ys[1]))
    y_diag = jnp.concatenate(pieces, axis=1)

    offs = []
    for g in range(SSD_GROUPS):
        b_g = bc_b[:, g * SSD_STATE:(g + 1) * SSD_STATE]
        c_g = bc_b[:, (SSD_GROUPS + g) * SSD_STATE:(SSD_GROUPS + g + 1) * SSD_STATE]
        gs = slice(g * SSD_GROUP_WIDTH, (g + 1) * SSD_GROUP_WIDTH)
        st = state_ref[g]
        offs.append(jnp.dot(c_g, st.astype(BF16), preferred_element_type=F32))
        state_ref[g] = st * ea_full[q - 1:q, gs] + lax.dot_general(
            b_g, xw_b[:, gs], _TN, preferred_element_type=F32)
    y_off = jnp.concatenate(offs, axis=1) * ea_full

    y = y_diag + y_off + dfull_ref[...] * xs
    y = y * _silu(z_ref[...].astype(F32))
    normed = []
    for g in range(SSD_GROUPS):
        yg = y[:, g * SSD_GROUP_WIDTH:(g + 1) * SSD_GROUP_WIDTH]
        normed.append(yg * lax.rsqrt(jnp.mean(yg * yg, axis=-1, keepdims=True) + EPS))
    y_ref[...] = (jnp.concatenate(normed, axis=1) * normw_ref[...]).astype(BF16)


def _ssd(main, small, smallT, conv_w, conv_b, dt_bias, a_log, d_skip, ssd_norm_w, fox_f_bias, *, q):
    L = main.shape[0]
    expand = (jnp.arange(SSD_INNER)[None, :] // SSD_HEAD_DIM == jnp.arange(SSD_HEADS)[:, None]).astype(BF16)
    const = lambda shape: pl.BlockSpec(shape, lambda c: (0,) * len(shape))
    return pl.pallas_call(
        functools.partial(_ssd_kernel, q=q),
        grid=(L // q,),
        in_specs=[
            pl.BlockSpec((q, SSD_INNER), lambda c: (c, COL_Z // SSD_INNER)),
            pl.BlockSpec((q, SSD_INNER), lambda c: (c, COL_XS // SSD_INNER)),
            pl.BlockSpec((q, SSD_BC), lambda c: (c, COL_BC // SSD_BC)),
            pl.BlockSpec((q, SMALL_DIM), lambda c: (c, 0)),
            pl.BlockSpec((SMALL_DIM, q), lambda c: (0, c)),
            const((SSD_CONV, SSD_INNER)), const((1, SSD_INNER)),
            const((SSD_CONV, SSD_BC)), const((1, SSD_BC)),
            const((1, SSD_HEADS)), const((SSD_HEADS, 1)),
            const((1, SSD_HEADS)), const((SSD_HEADS, 1)),
            const((1, FOX_HEADS)),
            const((1, SSD_INNER)), const((1, SSD_INNER)),
            const((SSD_HEADS, SSD_INNER)),
        ],
        out_specs=[
            pl.BlockSpec((q, SSD_INNER), lambda c: (c, 0)),
            pl.BlockSpec((q, FOX_HEADS), lambda c: (c, 0)),
        ],
        out_shape=[
            jax.ShapeDtypeStruct((L, SSD_INNER), BF16),
            jax.ShapeDtypeStruct((L, FOX_HEADS), F32),
        ],
        scratch_shapes=[
            pltpu.VMEM((q + 8, SSD_INNER), F32),
            pltpu.VMEM((q + 8, SSD_BC), F32),
            pltpu.VMEM((SSD_GROUPS, SSD_STATE, SSD_GROUP_WIDTH), F32),
            pltpu.VMEM((1, FOX_HEADS), F32),
        ],
        compiler_params=pltpu.CompilerParams(
            dimension_semantics=("arbitrary",), vmem_limit_bytes=VMEM_LIMIT),
        name="ssd",
    )(main, main, main, small, smallT,
      conv_w[:, :SSD_INNER], conv_b[:SSD_INNER].reshape(1, -1),
      conv_w[:, SSD_INNER:], conv_b[SSD_INNER:].reshape(1, -1),
      dt_bias.reshape(1, -1), dt_bias.reshape(-1, 1),
      a_log.reshape(1, -1), a_log.reshape(-1, 1),
      fox_f_bias.reshape(1, -1),
      jnp.repeat(d_skip, SSD_HEAD_DIM).reshape(1, -1), ssd_norm_w.reshape(1, -1),
      expand)


FOX_AUG = 3
FOX_SKIP_LOG2 = 160.0


def _foxprep_kernel(q_ref, k_ref, v_ref, cum_ref, kaug_ref, qaugT_ref, vaugT_ref, norm_ref):
    stat_row = lax.broadcasted_iota(jnp.int32, (8, LANES), 0)
    tb = q_ref.shape[0]
    lane = lax.broadcasted_iota(jnp.int32, (tb, LANES), 1)
    cum = cum_ref[...]
    rel = (cum[0:1, :] - cum) * LOG2E
    hrow = lax.broadcasted_iota(jnp.int32, (FOX_HEADS, FOX_INNER), 0)
    hcol = lax.broadcasted_iota(jnp.int32, (FOX_HEADS, FOX_INNER), 1)
    spare0 = (hrow // 2) * LANES + jnp.where(hrow % 2 == 0, FOX_HEAD_DIM, 0)
    aug_all = None
    for r, piece in enumerate(_split3(rel)):
        place = jnp.where(hcol == spare0 + r, 1.0, 0.0).astype(BF16)
        term = jnp.dot(piece, place, preferred_element_type=F32)
        aug_all = term if aug_all is None else aug_all + term
    aug_all = aug_all.astype(BF16)
    eye = jnp.where(lax.broadcasted_iota(jnp.int32, (LANES, LANES), 0)
                    == lax.broadcasted_iota(jnp.int32, (LANES, LANES), 1), 1.0, 0.0).astype(BF16)
    transpose = lambda xb: lax.dot_general(eye, xb, _NT, preferred_element_type=F32)
    trow = lax.broadcasted_iota(jnp.int32, (LANES, tb), 0)
    for p in range(FOX_HEADS // 2):
        cols = slice(p * LANES, (p + 1) * LANES)
        q = q_ref[:, cols].astype(F32) * (FOX_HEAD_DIM ** -0.5 * LOG2E)
        k = k_ref[:, cols]
        v = v_ref[:, cols].astype(F32)
        kT = transpose(k)
        stats = jnp.zeros((8, LANES), F32)
        for hh in range(2):
            head_lanes = (lane < FOX_HEAD_DIM) if hh == 0 else (lane >= FOX_HEAD_DIM)
            base = FOX_HEAD_DIM if hh == 0 else 0
            h = 2 * p + hh
            kaug_ref[h] = jnp.where(head_lanes, k, aug_all[:, cols])
            is_aug = (lane >= base) & (lane < base + FOX_AUG)
            qa = jnp.where(head_lanes, q, jnp.where(is_aug, 1.0, 0.0)).astype(BF16)
            va = jnp.where(head_lanes, v, jnp.where(lane == base, 1.0, 0.0)).astype(BF16)
            qaT = transpose(qa)
            qaugT_ref[h] = qaT.astype(BF16)
            vaugT_ref[h] = transpose(va).astype(BF16)
            head_rows = (trow < FOX_HEAD_DIM) if hh == 0 else (trow >= FOX_HEAD_DIM)
            sq_norms = lambda xT: jnp.sum(jnp.where(head_rows, xT * xT, 0.0), axis=0, keepdims=True)
            qn2 = jnp.max(sq_norms(qaT), axis=1, keepdims=True)
            kn2 = jnp.max(sq_norms(kT), axis=1, keepdims=True)
            stats = jnp.where(stat_row == hh, qn2, jnp.where(stat_row == 2 + hh, kn2, stats))
        norm_ref[p] = stats


def _foxprep(main, cum, *, tb):
    L = main.shape[0]
    pairs = FOX_HEADS // 2
    return pl.pallas_call(
        _foxprep_kernel,
        grid=(L // tb,),
        in_specs=[
            pl.BlockSpec((tb, FOX_INNER), lambda b: (b, COL_Q // FOX_INNER)),
            pl.BlockSpec((tb, FOX_INNER), lambda b: (b, COL_K // FOX_INNER)),
            pl.BlockSpec((tb, FOX_INNER), lambda b: (b, COL_V // FOX_INNER)),
            pl.BlockSpec((tb, FOX_HEADS), lambda b: (b, 0)),
        ],
        out_specs=[
            pl.BlockSpec((FOX_HEADS, tb, LANES), lambda b: (0, b, 0)),
            pl.BlockSpec((FOX_HEADS, LANES, tb), lambda b: (0, 0, b)),
            pl.BlockSpec((FOX_HEADS, LANES, tb), lambda b: (0, 0, b)),
            pl.BlockSpec((None, pairs, 8, LANES), lambda b: (b, 0, 0, 0)),
        ],
        out_shape=[
            jax.ShapeDtypeStruct((FOX_HEADS, L, LANES), BF16),
            jax.ShapeDtypeStruct((FOX_HEADS, LANES, L), BF16),
            jax.ShapeDtypeStruct((FOX_HEADS, LANES, L), BF16),
            jax.ShapeDtypeStruct((L // tb, pairs, 8, LANES), F32),
        ],
        compiler_params=pltpu.CompilerParams(
            dimension_semantics=("parallel",), vmem_limit_bytes=VMEM_LIMIT),
        name="foxprep",
    )(main, main, main, cum)


def _fox_kernel(tp_ref, ti_ref, tj_ref, first_ref, count_ref, cq_ref, ck_ref,
                kaug_ref, qaugT_ref, vaugT_ref, o_ref, m_ref, acc_ref, *, tq, tk):
    n = pl.program_id(0)
    p = tp_ref[n]
    i = ti_ref[n]
    j = tj_ref[n]
    live = n < count_ref[0]
    last = ((i + 1) * tq - 1) // tk

    @pl.when(jnp.logical_and(live, first_ref[n] == 1))
    def _():
        m_ref[...] = jnp.full(m_ref.shape, NEG_BIG, F32)
        acc_ref[...] = jnp.zeros(acc_ref.shape, F32)

    def step(masked, q_from):
        qs = slice(q_from, tq)
        if masked:
            key = j * tk + lax.broadcasted_iota(jnp.int32, (tk, tq - q_from), 0)
            qry = i * tq + q_from + lax.broadcasted_iota(jnp.int32, (tk, tq - q_from), 1)
            valid = key <= qry
        scores = [jnp.dot(kaug_ref[hh], qaugT_ref[hh, :, qs], preferred_element_type=F32) for hh in range(2)]
        for hh in range(2):
            h = 2 * p + hh
            d = cq_ref[i * FOX_HEADS + h] - ck_ref[j * FOX_HEADS + h]
            sT = scores[hh]
            if masked:
                sT = jnp.where(valid, sT, NEG_BIG)
            m_prev = m_ref[hh, :, qs]
            m_new = jnp.maximum(m_prev, jnp.max(sT, axis=0, keepdims=True) + d)
            alpha = jnp.exp2(m_prev - m_new)
            pT = jnp.exp2((sT - (m_new - d)).astype(BF16))
            acc_ref[hh, :, qs] = alpha * acc_ref[hh, :, qs] + jnp.dot(
                vaugT_ref[hh], pT, preferred_element_type=F32)
            m_ref[hh, :, qs] = m_new

    first_seeing = j * tk - i * tq
    crosses_diagonal = (j + 1) * tk - 1 > i * tq
    half = tq // 2
    assert half % LANES == 0
    upper_half_only = first_seeing >= half
    pl.when(live & crosses_diagonal & upper_half_only)(lambda: step(True, half))
    pl.when(live & crosses_diagonal & jnp.logical_not(upper_half_only))(lambda: step(True, 0))
    pl.when(live & jnp.logical_not(crosses_diagonal))(lambda: step(False, 0))

    @pl.when(jnp.logical_and(live, j == last))
    def _():
        a0 = acc_ref[0]
        a1 = acc_ref[1]
        o0 = a0[0:FOX_HEAD_DIM, :] / a0[FOX_HEAD_DIM:FOX_HEAD_DIM + 1, :]
        o1 = a1[FOX_HEAD_DIM:, :] / a1[0:1, :]
        o_ref[...] = jnp.concatenate([o0, o1], axis=0).T.astype(BF16)


def _fox_tables(cum, norms, *, tq, tk):
    L = cum.shape[0]
    nq, nk = L // tq, L // tk
    pairs = FOX_HEADS // 2
    assert nq < 256 and nk < 256
    pp, ii, jj = np.meshgrid(np.arange(pairs), np.arange(nq), np.arange(nk), indexing="ij")
    causal = jj * tk <= (ii + 1) * tq - 1
    diagonal = (jj + 1) * tk - 1 > ii * tq
    code = jnp.asarray((pp << 16 | ii << 8 | jj).reshape(-1), jnp.int32)
    n_steps = int(causal.sum())

    qn = jnp.sqrt(norms[:, :, 0:2, 0]).reshape(nk, FOX_HEADS)
    kn = jnp.sqrt(norms[:, :, 2:4, 0]).reshape(nk, FOX_HEADS)
    qn_i = qn.reshape(nq, tq // tk, FOX_HEADS).max(axis=1)
    kd_i = kn.reshape(nq, tq // tk, FOX_HEADS).max(axis=1)
    cq = cum[::tq] * LOG2E
    ck = cum[::tk] * LOG2E
    c_last = cum[tk - 1::tk] * LOG2E
    bound = qn_i[:, None, :] * (kn[None, :, :] + kd_i[:, None, :]) + (cq[:, None, :] - c_last[None, :, :])
    alive = (bound >= -FOX_SKIP_LOG2).reshape(nq, nk, pairs, 2).any(axis=-1)
    live = (jnp.asarray(causal) & (jnp.asarray(diagonal) | jnp.transpose(alive, (2, 0, 1)))).reshape(-1)
    pos = jnp.cumsum(live.astype(jnp.int32)) - 1
    count = pos[-1] + 1
    slot = jnp.arange(n_steps, dtype=jnp.int32)
    hit = live[None, :] & (pos[None, :] == slot[:, None])
    picked = jnp.sum(jnp.where(hit, code[None, :], 0), axis=1)
    picked = jnp.where(slot < count, picked, jnp.max(jnp.where(live, code, -1)))
    tp, ti, tj = picked >> 16, (picked >> 8) & 255, picked & 255
    first = jnp.concatenate([jnp.ones((1,), jnp.int32),
                             ((tp[1:] != tp[:-1]) | (ti[1:] != ti[:-1])).astype(jnp.int32)])
    return tp, ti, tj, first, count.reshape(1), cq.reshape(-1), ck.reshape(-1)


def _fox(main, cum, *, tq, tk):
    L = main.shape[0]
    kaug, qaugT, vaugT, norms = _foxprep(main, cum, tb=tk)
    tables = _fox_tables(cum, norms, tq=tq, tk=tk)
    grid_spec = pltpu.PrefetchScalarGridSpec(
        num_scalar_prefetch=len(tables),
        grid=(tables[0].shape[0],),
        in_specs=[
            pl.BlockSpec((2, tk, LANES), lambda n, tp, ti, tj, *_: (tp[n], tj[n], 0)),
            pl.BlockSpec((2, LANES, tq), lambda n, tp, ti, tj, *_: (tp[n], 0, ti[n])),
            pl.BlockSpec((2, LANES, tk), lambda n, tp, ti, tj, *_: (tp[n], 0, tj[n])),
        ],
        out_specs=pl.BlockSpec((tq, LANES), lambda n, tp, ti, tj, *_: (ti[n], tp[n])),
        scratch_shapes=[
            pltpu.VMEM((2, 1, tq), F32),
            pltpu.VMEM((2, LANES, tq), F32),
        ],
    )
    return pl.pallas_call(
        functools.partial(_fox_kernel, tq=tq, tk=tk),
        grid_spec=grid_spec,
        out_shape=jax.ShapeDtypeStruct((L, FOX_INNER), BF16),
        compiler_params=pltpu.CompilerParams(
            dimension_semantics=("arbitrary",), vmem_limit_bytes=VMEM_LIMIT),
        name="fox",
    )(*tables, kaug, qaugT, vaugT)


def _mid_kernel(x_ref, ys_ref, yf_ref, wo_ref, ln2_ref, wq_ref, k1_ref, k2_ref,
                x1_ref, h2T_ref, s1T_ref, s2T_ref):
    mix = jnp.dot(ys_ref[...], wo_ref[0:SSD_INNER, :], preferred_element_type=F32)
    mix = mix + jnp.dot(yf_ref[...], wo_ref[SSD_INNER:, :], preferred_element_type=F32)
    x1 = x_ref[...] + mix
    x1_ref[...] = x1
    h2 = x1 * lax.rsqrt(jnp.mean(x1 * x1, axis=-1, keepdims=True) + EPS) * ln2_ref[...]
    h2T_ref[...] = h2.T.astype(BF16)
    qp = jnp.dot(h2.astype(BF16), wq_ref[...], preferred_element_type=F32)
    half = PEER_KEY_DIM // 2
    for h in range(PEER_HEADS):
        q1 = qp[:, h * PEER_KEY_DIM:h * PEER_KEY_DIM + half].astype(BF16)
        q2 = qp[:, h * PEER_KEY_DIM + half:(h + 1) * PEER_KEY_DIM].astype(BF16)
        s1T_ref[h] = lax.dot_general(k1_ref[h], q1, _NT, preferred_element_type=F32)
        s2T_ref[h] = lax.dot_general(k2_ref[h], q2, _NT, preferred_element_type=F32)


def _mid(x2d, y_ssd, y_fox, w_out, ln2_w, wq, k1, k2, *, tm):
    L = x2d.shape[0]
    half = PEER_KEY_DIM // 2
    resident = lambda shape: pl.BlockSpec(shape, lambda i: (0,) * len(shape), pipeline_mode=pl.Buffered(1))
    return pl.pallas_call(
        _mid_kernel,
        grid=(L // tm,),
        in_specs=[
            pl.BlockSpec((tm, D_MODEL), lambda i: (i, 0)),
            pl.BlockSpec((tm, SSD_INNER), lambda i: (i, 0)),
            pl.BlockSpec((tm, FOX_INNER), lambda i: (i, 0)),
            resident((D_MODEL, D_MODEL)),
            resident((1, D_MODEL)),
            resident((D_MODEL, PEER_HEADS * PEER_KEY_DIM)),
            resident((PEER_HEADS, PEER_N_KEYS, half)),
            resident((PEER_HEADS, PEER_N_KEYS, half)),
        ],
        out_specs=[
            pl.BlockSpec((tm, D_MODEL), lambda i: (i, 0)),
            pl.BlockSpec((D_MODEL, tm), lambda i: (0, i)),
            pl.BlockSpec((PEER_HEADS, PEER_N_KEYS, tm), lambda i: (0, 0, i)),
            pl.BlockSpec((PEER_HEADS, PEER_N_KEYS, tm), lambda i: (0, 0, i)),
        ],
        out_shape=[
            jax.ShapeDtypeStruct((L, D_MODEL), F32),
            jax.ShapeDtypeStruct((D_MODEL, L), BF16),
            jax.ShapeDtypeStruct((PEER_HEADS, PEER_N_KEYS, L), F32),
            jax.ShapeDtypeStruct((PEER_HEADS, PEER_N_KEYS, L), F32),
        ],
        compiler_params=pltpu.CompilerParams(
            dimension_semantics=("parallel",), vmem_limit_bytes=VMEM_LIMIT),
        name="mid",
    )(x2d, y_ssd, y_fox, w_out, ln2_w.reshape(1, D_MODEL), wq, k1, k2)


def _top16(s, order, exact):
    axis = s.ndim - 2
    slot = lax.broadcasted_iota(jnp.int32, s.shape[:-2] + (PEER_TOPK, s.shape[-1]), axis)
    unranked = float(PEER_TOPK)

    if not exact:
        def next_value(r, carry):
            below, vals = carry
            m = jnp.max(jnp.where(s < below, s, -jnp.inf), axis=axis, keepdims=True)
            return m, jnp.where(slot == r, m, vals)

        ceiling = jnp.full(s.shape[:-2] + (1, s.shape[-1]), jnp.inf, F32)
        lowest, vals = lax.fori_loop(0, PEER_TOPK, next_value, (ceiling, jnp.zeros(slot.shape, F32)))
        taken = s >= lowest
        return None, vals, taken, jnp.sum(jnp.where(taken, 1.0, 0.0), axis=axis, keepdims=True)

    def extract(r, carry):
        s, rank, vals = carry
        m = jnp.max(s, axis=axis, keepdims=True)
        first = jnp.min(jnp.where(s == m, order, jnp.inf), axis=axis, keepdims=True)
        hit = order == first
        rank = jnp.where(hit, lax.convert_element_type(r, F32), rank)
        s = jnp.where(hit, -jnp.inf, s)
        return s, rank, jnp.where(slot == r, m, vals)

    init = (s, jnp.full(s.shape, unranked, F32), jnp.zeros(slot.shape, F32))
    _, rank, vals = lax.fori_loop(0, PEER_TOPK, extract, init)
    taken = rank < unranked
    return rank, vals, taken, jnp.sum(jnp.where(taken, 1.0, 0.0), axis=axis, keepdims=True)


def _topk_kernel(s1T_ref, s2T_ref, lim_ref, e1_ref, r2_ref, e2_ref):
    t = s1T_ref.shape[2]
    key_id = lax.broadcasted_iota(jnp.int32, (PEER_N_KEYS, t), 0).astype(F32)
    half_k = PEER_TOPK // 2
    pos_main = lax.broadcasted_iota(jnp.int32, (half_k * PEER_TOPK, t), 0)
    pos_tail = (lax.broadcasted_iota(jnp.int32, (half_k, t), 0) + half_k) * PEER_TOPK
    pos = jnp.concatenate([pos_main, pos_tail], axis=0).astype(F32)
    slot = lax.broadcasted_iota(jnp.int32, (PEER_TOPK, t), 0)

    def solve(h, exact):
        s1 = s1T_ref[h]
        s2 = s2T_ref[h]
        rank12, v12, _, n12 = _top16(jnp.stack([s1, s2]), key_id, exact)
        v1, v2 = v12[0], v12[1]

        cand = jnp.concatenate(
            [v1[a:a + 1, :] + v2 for a in range(half_k)] + [v1[half_k:, :] + v2[0:1, :]], axis=0)
        _, top_sums, taken_c, n_c = _top16(cand, pos, exact)
        sel = jnp.where(taken_c, 1.0, 0.0)
        z = jnp.sum(jnp.exp(top_sums - top_sums[0:1, :]), axis=0, keepdims=True)
        n_taken = jnp.maximum(jnp.maximum(n12[0], n12[1]), n_c)

        if exact:
            rank1, rank2 = rank12[0], rank12[1]
            is_rank1 = lambda a: rank1 == float(a)
        else:
            rank2 = jnp.full(s2.shape, float(PEER_TOPK), F32)
            for a in range(PEER_TOPK):
                rank2 = jnp.where(s2 == v2[a:a + 1, :], float(a), rank2)
            is_rank1 = lambda a: s1 == v1[a:a + 1, :]

        bcount = jnp.zeros((PEER_TOPK, t), F32)
        for a in range(half_k):
            cnt = jnp.sum(sel[a * PEER_TOPK:(a + 1) * PEER_TOPK, :], axis=0, keepdims=True)
            bcount = jnp.where(slot == a, cnt, bcount)
        tail = jnp.concatenate([jnp.zeros((half_k, t), F32), sel[half_k * PEER_TOPK:, :]], axis=0)
        bcount = jnp.where(slot >= half_k, tail, bcount)

        lim = jnp.zeros((PEER_N_KEYS, t), F32)
        for a in range(PEER_TOPK):
            lim = jnp.where(is_rank1(a), bcount[a:a + 1, :], lim)

        lim_ref[h] = lim
        e1_ref[h] = jnp.exp(s1 - v1[0:1, :])
        r2_ref[h] = rank2.astype(BF16)
        e2_ref[h] = (jnp.exp(s2 - v2[0:1, :]) / z).astype(BF16)
        return jnp.max(n_taken)

    def head(h, _):
        most = solve(h, exact=False)

        @pl.when(most > float(PEER_TOPK))
        def _():
            solve(h, exact=True)

        return 0

    lax.fori_loop(0, PEER_HEADS, head, 0)


def _topk(s1T, s2T, *, tt):
    L = s1T.shape[2]
    spec = pl.BlockSpec((PEER_HEADS, PEER_N_KEYS, tt), lambda i: (0, 0, i))
    shape = jax.ShapeDtypeStruct((PEER_HEADS, PEER_N_KEYS, L), F32)
    shape_b = jax.ShapeDtypeStruct((PEER_HEADS, PEER_N_KEYS, L), BF16)
    return pl.pallas_call(
        _topk_kernel,
        grid=(L // tt,),
        in_specs=[spec, spec],
        out_specs=[spec] * 4,
        out_shape=[shape, shape, shape_b, shape_b],
        compiler_params=pltpu.CompilerParams(
            dimension_semantics=("parallel",), vmem_limit_bytes=VMEM_LIMIT),
        name="topk",
    )(s1T, s2T)


def _peer_kernel(h2T_ref, u_ref, unext_ref, vT_ref, lim_ref, e1_ref, r2_ref, e2_ref, x1_ref, lnf_ref,
                 o_ref, acc_ref, act0_ref, *, et, sub):
    e = pl.program_id(1)
    n_sub = et // sub
    keys_per_sub = sub // PEER_N_KEYS

    def activation(rows_ref, s):
        return jnp.dot(rows_ref[s * sub:(s + 1) * sub, :], h2T_ref[...], preferred_element_type=F32)

    @pl.when(e == 0)
    def _():
        acc_ref[...] = jnp.zeros(acc_ref.shape, F32)
        act0_ref[...] = activation(u_ref, 0)

    def gate(s):
        parts = []
        for i in range(keys_per_sub):
            i1 = e * (et // PEER_N_KEYS) + s * keys_per_sub + i
            w = None
            for h in range(PEER_HEADS):
                lim_row = lim_ref[h, pl.ds(i1, 1), :].astype(BF16)
                e1_row = e1_ref[h, pl.ds(i1, 1), :].astype(BF16)
                term = jnp.where(r2_ref[h] < lim_row, e2_ref[h] * e1_row, jnp.zeros((), BF16))
                w = term if w is None else w + term
            parts.append(w)
        return jnp.concatenate(parts, axis=0)

    total = None
    a = act0_ref[...]
    for s in range(n_sub):
        a_next = activation(u_ref, s + 1) if s + 1 < n_sub else activation(unext_ref, 0)
        gelu = 0.5 * a * (1.0 + lax.erf(a * (2.0 ** -0.5)))
        wg = gelu.astype(BF16) * gate(s)
        part = jnp.dot(vT_ref[:, s * sub:(s + 1) * sub], wg, preferred_element_type=F32)
        total = part if total is None else total + part
        a = a_next
    act0_ref[...] = a
    acc_ref[...] += total

    @pl.when(e == pl.num_programs(1) - 1)
    def _():
        x2 = x1_ref[...] + acc_ref[...].T
        o_ref[...] = x2 * lax.rsqrt(jnp.mean(x2 * x2, axis=-1, keepdims=True) + EPS) * lnf_ref[...]


def _peer(h2T, u_b, vT_b, lim, e1, r2, e2, x1, lnf_w, *, tt, et, sub):
    L = x1.shape[0]
    tok = pl.BlockSpec((PEER_HEADS, PEER_N_KEYS, tt), lambda t, e: (0, 0, t), pipeline_mode=pl.Buffered(1))
    n_blocks = PEER_N_EXPERTS // et
    return pl.pallas_call(
        functools.partial(_peer_kernel, et=et, sub=sub),
        grid=(L // tt, n_blocks),
        in_specs=[
            pl.BlockSpec((D_MODEL, tt), lambda t, e: (0, t), pipeline_mode=pl.Buffered(1)),
            pl.BlockSpec((et, D_MODEL), lambda t, e: (e, 0)),
            pl.BlockSpec((sub, D_MODEL), lambda t, e: (jnp.minimum(e + 1, n_blocks - 1) * (et // sub), 0)),
            pl.BlockSpec((D_MODEL, et), lambda t, e: (0, e)),
            tok, tok, tok, tok,
            pl.BlockSpec((tt, D_MODEL), lambda t, e: (t, 0), pipeline_mode=pl.Buffered(1)),
            pl.BlockSpec((1, D_MODEL), lambda t, e: (0, 0)),
        ],
        out_specs=pl.BlockSpec((tt, D_MODEL), lambda t, e: (t, 0)),
        out_shape=jax.ShapeDtypeStruct((L, D_MODEL), F32),
        scratch_shapes=[
            pltpu.VMEM((D_MODEL, tt), F32),
            pltpu.VMEM((sub, tt), F32),
        ],
        compiler_params=pltpu.CompilerParams(
            dimension_semantics=("parallel", "arbitrary"), vmem_limit_bytes=VMEM_LIMIT),
        name="peer",
    )(h2T, u_b, u_b, vT_b, lim, e1, r2, e2, x1, lnf_w.reshape(1, D_MODEL))


def _tiles(L):
    return dict(
        inproj_tm=min(512, L), inproj_tn=2816,
        ssd_q=min(128, L),
        fox_tq=min(1024, L), fox_tk=min(512, L),
        mid_tm=min(256, L),
        topk_tt=min(256, L),
        peer_tt=min(512, L), peer_et=1024, peer_sub=256,
    )


def kernel(x, ln1_w, w_in, conv_w, conv_b, dt_bias, a_log, d_skip, ssd_norm_w, fox_f_bias,
           w_out, ln2_w, peer_wq, peer_k1, peer_k2, peer_u, peer_v, lnf_w):
    B, L, D = x.shape
    assert B == 1 and D == D_MODEL
    t = _tiles(L)
    x2d = x.reshape(L, D)

    c_z, c_xbc, c_dt, c_q = 0, SSD_INNER, 2560, 2576
    c_f = c_q + 3 * FOX_INNER
    w_main = jnp.concatenate([w_in[:, c_q:c_f], w_in[:, c_z:c_dt]], axis=1).astype(BF16)
    w_small = jnp.concatenate(
        [w_in[:, c_dt:c_q], w_in[:, c_f:], jnp.zeros((D, SMALL_DIM - SSD_HEADS - FOX_HEADS), F32)],
        axis=1).astype(BF16)

    main, small, smallT = _inproj(x2d, ln1_w, w_main, w_small, tm=t["inproj_tm"], tn=t["inproj_tn"])
    y_ssd, cum = _ssd(main, small, smallT, conv_w, conv_b, dt_bias, a_log, d_skip, ssd_norm_w,
                      fox_f_bias, q=t["ssd_q"])
    y_fox = _fox(main, cum, tq=t["fox_tq"], tk=t["fox_tk"])
    x1, h2T, s1T, s2T = _mid(x2d, y_ssd, y_fox, w_out.astype(BF16), ln2_w, peer_wq.astype(BF16),
                             peer_k1.astype(BF16), peer_k2.astype(BF16), tm=t["mid_tm"])
    lim, e1, r2, e2 = _topk(s1T, s2T, tt=t["topk_tt"])
    out = _peer(h2T, peer_u.astype(BF16), peer_v.T.astype(BF16), lim, e1, r2, e2, x1, lnf_w,
                tt=t["peer_tt"], et=t["peer_et"], sub=t["peer_sub"])
    return out.reshape(B, L, D)
```

```python
import functools

import numpy as np
import jax
import jax.numpy as jnp
from jax import lax
from jax.experimental import pallas as pl
from jax.experimental.pallas import tpu as pltpu

F32 = jnp.float32
BF16 = jnp.bfloat16

D_MODEL = 2048
EPS = 1e-6

SSD_HEAD_DIM = 64
SSD_INNER = 1024
SSD_HEADS = 16
SSD_GROUPS = 2
SSD_STATE = 128
SSD_CONV = 4
SSD_TAIL = 16
SSD_BC = 2 * SSD_GROUPS * SSD_STATE
SSD_GROUP_WIDTH = SSD_INNER // SSD_GROUPS

FOX_HEAD_DIM = 64
FOX_INNER = 1024
FOX_HEADS = 16

PEER_HEADS = 8
PEER_N_KEYS = 128
PEER_N_EXPERTS = PEER_N_KEYS * PEER_N_KEYS
PEER_KEY_DIM = 256
PEER_TOPK = 16

COL_Q = 0
COL_K = 1024
COL_V = 2048
COL_Z = 3072
COL_XS = 4096
COL_BC = 5120
MAIN_DIM = 5632
SMALL_DIM = 128

LANES = 128
VMEM_LIMIT = 56 * 1024 * 1024

NEG_BIG = -1e30
LOG2E = 1.4426950408889634

_NT = (((1,), (1,)), ((), ()))
_TN = (((0,), (0,)), ((), ()))


def _softplus(x):
    return jnp.maximum(x, 0.0) + jnp.log1p(jnp.exp(-jnp.abs(x)))


def _silu(x):
    return x * (0.5 * jnp.tanh(0.5 * x) + 0.5)


def _split3(x):
    hi = x.astype(BF16)
    r = x - hi.astype(F32)
    mid = r.astype(BF16)
    lo = (r - mid.astype(F32)).astype(BF16)
    return hi, mid, lo


def _dot3_lhs(x, w):
    a, b, c = _split3(x)
    d = lambda p: jnp.dot(p, w, preferred_element_type=F32)
    return d(a) + d(b) + d(c)


def _dot3_rhs(w, x):
    a, b, c = _split3(x)
    d = lambda p: jnp.dot(w, p, preferred_element_type=F32)
    return d(a) + d(b) + d(c)


def _inproj_kernel(x_ref, lnw_ref, wm_ref, ws_ref, wsT_ref, main_ref, small_ref, smallT_ref, h_ref):
    @pl.when(pl.program_id(1) == 0)
    def _():
        x = x_ref[...]
        h = x * lax.rsqrt(jnp.mean(x * x, axis=-1, keepdims=True) + EPS) * lnw_ref[...]
        hb = h.astype(BF16)
        h_ref[...] = hb
        small_ref[...] = jnp.dot(hb, ws_ref[...], preferred_element_type=F32)
        smallT_ref[...] = lax.dot_general(wsT_ref[...], hb, _NT, preferred_element_type=F32)

    main_ref[...] = jnp.dot(h_ref[...], wm_ref[...], preferred_element_type=F32).astype(BF16)


def _inproj(x2d, ln1_w, w_main, w_small, *, tm, tn):
    L = x2d.shape[0]
    return pl.pallas_call(
        _inproj_kernel,
        grid=(L // tm, MAIN_DIM // tn),
        in_specs=[
            pl.BlockSpec((tm, D_MODEL), lambda i, j: (i, 0)),
            pl.BlockSpec((1, D_MODEL), lambda i, j: (0, 0)),
            pl.BlockSpec((D_MODEL, tn), lambda i, j: (0, j)),
            pl.BlockSpec((D_MODEL, SMALL_DIM), lambda i, j: (0, 0)),
            pl.BlockSpec((SMALL_DIM, D_MODEL), lambda i, j: (0, 0)),
        ],
        out_specs=[
            pl.BlockSpec((tm, tn), lambda i, j: (i, j)),
            pl.BlockSpec((tm, SMALL_DIM), lambda i, j: (i, 0)),
            pl.BlockSpec((SMALL_DIM, tm), lambda i, j: (0, i)),
        ],
        out_shape=[
            jax.ShapeDtypeStruct((L, MAIN_DIM), BF16),
            jax.ShapeDtypeStruct((L, SMALL_DIM), F32),
            jax.ShapeDtypeStruct((SMALL_DIM, L), F32),
        ],
        scratch_shapes=[pltpu.VMEM((tm, D_MODEL), BF16)],
        compiler_params=pltpu.CompilerParams(
            dimension_semantics=("parallel", "arbitrary"), vmem_limit_bytes=VMEM_LIMIT),
        name="inproj",
    )(x2d, ln1_w.reshape(1, D_MODEL), w_main, w_small, w_small.T)


def _ssd_kernel(z_ref, xs_ref, bc_ref, small_ref, smallT_ref,
                cwx_ref, cbx_ref, cwbc_ref, cbbc_ref,
                dtb_ref, dtbT_ref, alog_ref, alogT_ref, fb_ref,
                dfull_ref, normw_ref, expand_ref,
                y_ref, cum_ref,
                extx_ref, extbc_ref, state_ref, fcarry_ref, *, q):
    c = pl.program_id(0)
    tail = SSD_TAIL

    @pl.when(c == 0)
    def _():
        extx_ref[0:tail, :] = jnp.zeros((tail, SSD_INNER), F32)
        extbc_ref[0:tail, :] = jnp.zeros((tail, SSD_BC), F32)
        state_ref[...] = jnp.zeros(state_ref.shape, F32)
        fcarry_ref[...] = jnp.zeros(fcarry_ref.shape, F32)

    extx_ref[tail:tail + q, :] = xs_ref[...].astype(F32)
    extbc_ref[tail:tail + q, :] = bc_ref[...].astype(F32)

    srow = lax.broadcasted_iota(jnp.int32, (q, q + tail), 0)
    scol = lax.broadcasted_iota(jnp.int32, (q, q + tail), 1)

    def conv(ext_ref, w_ref, b_ref):
        acc = b_ref[...] + ext_ref[tail:tail + q, :] * w_ref[SSD_CONV - 1:SSD_CONV, :]
        ext = ext_ref[...].astype(BF16)
        for k in range(SSD_CONV - 1):
            off = tail - (SSD_CONV - 1) + k
            shift = jnp.where(scol == srow + off, 1.0, 0.0).astype(BF16)
            acc = acc + jnp.dot(shift, ext, preferred_element_type=F32) * w_ref[k:k + 1, :]
        return acc

    xs = _silu(conv(extx_ref, cwx_ref, cbx_ref))
    bc = _silu(conv(extbc_ref, cwbc_ref, cbbc_ref))
    extx_ref[0:tail, :] = extx_ref[q:q + tail, :]
    extbc_ref[0:tail, :] = extbc_ref[q:q + tail, :]

    row = lax.broadcasted_iota(jnp.int32, (q, q), 0)
    col = lax.broadcasted_iota(jnp.int32, (q, q), 1)
    lower = row >= col
    tri = lower.astype(BF16)
    upper = (row <= col).astype(BF16)

    small = small_ref[...]
    smallT = smallT_ref[...]
    neg_a = -jnp.exp(alog_ref[...])
    neg_aT = -jnp.exp(alogT_ref[...])
    dt = _softplus(small[:, 0:SSD_HEADS] + dtb_ref[...])
    dtT = _softplus(smallT[0:SSD_HEADS, :] + dtbT_ref[...])
    a_cum = _dot3_rhs(tri, dt * neg_a)
    a_cumT = _dot3_lhs(dtT * neg_aT, upper)
    a_last = a_cum[q - 1:q, :]

    logf = -_softplus(-(small[:, SSD_HEADS:SSD_HEADS + FOX_HEADS] + fb_ref[...]))
    cumf = fcarry_ref[...] + _dot3_rhs(tri, logf)
    cum_ref[...] = cumf
    fcarry_ref[...] = cumf[q - 1:q, :]

    expand = expand_ref[...]
    dt_full = _dot3_lhs(dt, expand)
    ea_full = _dot3_lhs(jnp.exp(a_cum), expand)
    ds_full = _dot3_lhs(jnp.exp(a_last - a_cum), expand)

    xdt = xs * dt_full
    xdt_b = xdt.astype(BF16)
    xw_b = (xdt * ds_full).astype(BF16)
    bc_b = bc.astype(BF16)
    lane = lax.broadcasted_iota(jnp.int32, (q, LANES), 1)

    pieces = []
    for g in range(SSD_GROUPS):
        b_g = bc_b[:, g * SSD_STATE:(g + 1) * SSD_STATE]
        c_g = bc_b[:, (SSD_GROUPS + g) * SSD_STATE:(SSD_GROUPS + g + 1) * SSD_STATE]
        cb = lax.dot_general(c_g, b_g, _NT, preferred_element_type=F32)
        heads_per_group = SSD_HEADS // SSD_GROUPS
        for pp in range(heads_per_group // 2):
            pair = g * (heads_per_group // 2) + pp
            xp = xdt_b[:, pair * LANES:(pair + 1) * LANES]
            ys = []
            for hh in range(2):
                h = 2 * pair + hh
                seg = a_cum[:, h:h + 1] - a_cumT[h:h + 1, :]
                decay = jnp.exp(jnp.where(lower, seg, -jnp.inf))
                m = (cb * decay).astype(BF16)
                ys.append(jnp.dot(m, xp, preferred_element_type=F32))
            pieces.append(jnp.where(lane < SSD_HEAD_DIM, ys[0], ys[1]))
    y_diag = jnp.concatenate(pieces, axis=1)

    offs = []
    for g in range(SSD_GROUPS):
        b_g = bc_b[:, g * SSD_STATE:(g + 1) * SSD_STATE]
        c_g = bc_b[:, (SSD_GROUPS + g) * SSD_STATE:(SSD_GROUPS + g + 1) * SSD_STATE]
        gs = slice(g * SSD_GROUP_WIDTH, (g + 1) * SSD_GROUP_WIDTH)
        st = state_ref[g]
        offs.append(jnp.dot(c_g, st.astype(BF16), preferred_element_type=F32))
        state_ref[g] = st * ea_full[q - 1:q, gs] + lax.dot_general(
            b_g, xw_b[:, gs], _TN, preferred_element_type=F32)
    y_off = jnp.concatenate(offs, axis=1) * ea_full

    y = y_diag + y_off + dfull_ref[...] * xs
    y = y * _silu(z_ref[...].astype(F32))
    normed = []
    for g in range(SSD_GROUPS):
        yg = y[:, g * SSD_GROUP_WIDTH:(g + 1) * SSD_GROUP_WIDTH]
        normed.append(yg * lax.rsqrt(jnp.mean(yg * yg, axis=-1, keepdims=True) + EPS))
    y_ref[...] = (jnp.concatenate(normed, axis=1) * normw_ref[...]).astype(BF16)


def _ssd(main, small, smallT, conv_w, conv_b, dt_bias, a_log, d_skip, ssd_norm_w, fox_f_bias, *, q):
    L = main.shape[0]
    expand = (jnp.arange(SSD_INNER)[None, :] // SSD_HEAD_DIM == jnp.arange(SSD_HEADS)[:, None]).astype(BF16)
    const = lambda shape: pl.BlockSpec(shape, lambda c: (0,) * len(shape))
    return pl.pallas_call(
        functools.partial(_ssd_kernel, q=q),
        grid=(L // q,),
        in_specs=[
            pl.BlockSpec((q, SSD_INNER), lambda c: (c, COL_Z // SSD_INNER)),
            pl.BlockSpec((q, SSD_INNER), lambda c: (c, COL_XS // SSD_INNER)),
            pl.BlockSpec((q, SSD_BC), lambda c: (c, COL_BC // SSD_BC)),
            pl.BlockSpec((q, SMALL_DIM), lambda c: (c, 0)),
            pl.BlockSpec((SMALL_DIM, q), lambda c: (0, c)),
            const((SSD_CONV, SSD_INNER)), const((1, SSD_INNER)),
            const((SSD_CONV, SSD_BC)), const((1, SSD_BC)),
            const((1, SSD_HEADS)), const((SSD_HEADS, 1)),
            const((1, SSD_HEADS)), const((SSD_HEADS, 1)),
            const((1, FOX_HEADS)),
            const((1, SSD_INNER)), const((1, SSD_INNER)),
            const((SSD_HEADS, SSD_INNER)),
        ],
        out_specs=[
            pl.BlockSpec((q, SSD_INNER), lambda c: (c, 0)),
            pl.BlockSpec((q, FOX_HEADS), lambda c: (c, 0)),
        ],
        out_shape=[
            jax.ShapeDtypeStruct((L, SSD_INNER), BF16),
            jax.ShapeDtypeStruct((L, FOX_HEADS), F32),
        ],
        scratch_shapes=[
            pltpu.VMEM((q + SSD_TAIL, SSD_INNER), F32),
            pltpu.VMEM((q + SSD_TAIL, SSD_BC), F32),
            pltpu.VMEM((SSD_GROUPS, SSD_STATE, SSD_GROUP_WIDTH), F32),
            pltpu.VMEM((1, FOX_HEADS), F32),
        ],
        compiler_params=pltpu.CompilerParams(
            dimension_semantics=("arbitrary",), vmem_limit_bytes=VMEM_LIMIT),
        name="ssd",
    )(main, main, main, small, smallT,
      conv_w[:, :SSD_INNER], conv_b[:SSD_INNER].reshape(1, -1),
      conv_w[:, SSD_INNER:], conv_b[SSD_INNER:].reshape(1, -1),
      dt_bias.reshape(1, -1), dt_bias.reshape(-1, 1),
      a_log.reshape(1, -1), a_log.reshape(-1, 1),
      fox_f_bias.reshape(1, -1),
      jnp.repeat(d_skip, SSD_HEAD_DIM).reshape(1, -1), ssd_norm_w.reshape(1, -1),
      expand)


FOX_AUG = 3
FOX_SKIP_LOG2 = 160.0


def _foxprep_kernel(q_ref, k_ref, v_ref, cum_ref, kaug_ref, qaugT_ref, vaugT_ref, norm_ref):
    stat_row = lax.broadcasted_iota(jnp.int32, (8, LANES), 0)
    tb = q_ref.shape[0]
    lane = lax.broadcasted_iota(jnp.int32, (tb, LANES), 1)
    cum = cum_ref[...]
    rel = (cum[0:1, :] - cum) * LOG2E
    hrow = lax.broadcasted_iota(jnp.int32, (FOX_HEADS, FOX_INNER), 0)
    hcol = lax.broadcasted_iota(jnp.int32, (FOX_HEADS, FOX_INNER), 1)
    spare0 = (hrow // 2) * LANES + jnp.where(hrow % 2 == 0, FOX_HEAD_DIM, 0)
    aug_all = None
    for r, piece in enumerate(_split3(rel)):
        place = jnp.where(hcol == spare0 + r, 1.0, 0.0).astype(BF16)
        term = jnp.dot(piece, place, preferred_element_type=F32)
        aug_all = term if aug_all is None else aug_all + term
    aug_all = aug_all.astype(BF16)
    eye = jnp.where(lax.broadcasted_iota(jnp.int32, (LANES, LANES), 0)
                    == lax.broadcasted_iota(jnp.int32, (LANES, LANES), 1), 1.0, 0.0).astype(BF16)
    transpose = lambda xb: lax.dot_general(eye, xb, _NT, preferred_element_type=F32)
    trow = lax.broadcasted_iota(jnp.int32, (LANES, tb), 0)
    for p in range(FOX_HEADS // 2):
        cols = slice(p * LANES, (p + 1) * LANES)
        q = q_ref[:, cols].astype(F32) * (FOX_HEAD_DIM ** -0.5 * LOG2E)
        k = k_ref[:, cols]
        v = v_ref[:, cols].astype(F32)
        kT = transpose(k)
        stats = jnp.zeros((8, LANES), F32)
        for hh in range(2):
            head_lanes = (lane < FOX_HEAD_DIM) if hh == 0 else (lane >= FOX_HEAD_DIM)
            base = FOX_HEAD_DIM if hh == 0 else 0
            h = 2 * p + hh
            kaug_ref[h] = jnp.where(head_lanes, k, aug_all[:, cols])
            is_aug = (lane >= base) & (lane < base + FOX_AUG)
            qa = jnp.where(head_lanes, q, jnp.where(is_aug, 1.0, 0.0)).astype(BF16)
            va = jnp.where(head_lanes, v, jnp.where(lane == base, 1.0, 0.0)).astype(BF16)
            qaT = transpose(qa)
            qaugT_ref[h] = qaT.astype(BF16)
            vaugT_ref[h] = transpose(va).astype(BF16)
            head_rows = (trow < FOX_HEAD_DIM) if hh == 0 else (trow >= FOX_HEAD_DIM)
            sq_norms = lambda xT: jnp.sum(jnp.where(head_rows, xT * xT, 0.0), axis=0, keepdims=True)
            qn2 = jnp.max(sq_norms(qaT), axis=1, keepdims=True)
            kn2 = jnp.max(sq_norms(kT), axis=1, keepdims=True)
            stats = jnp.where(stat_row == hh, qn2, jnp.where(stat_row == 2 + hh, kn2, stats))
        norm_ref[p] = stats


def _foxprep(main, cum, *, tb):
    L = main.shape[0]
    pairs = FOX_HEADS // 2
    return pl.pallas_call(
        _foxprep_kernel,
        grid=(L // tb,),
        in_specs=[
            pl.BlockSpec((tb, FOX_INNER), lambda b: (b, COL_Q // FOX_INNER)),
            pl.BlockSpec((tb, FOX_INNER), lambda b: (b, COL_K // FOX_INNER)),
            pl.BlockSpec((tb, FOX_INNER), lambda b: (b, COL_V // FOX_INNER)),
            pl.BlockSpec((tb, FOX_HEADS), lambda b: (b, 0)),
        ],
        out_specs=[
            pl.BlockSpec((FOX_HEADS, tb, LANES), lambda b: (0, b, 0)),
            pl.BlockSpec((FOX_HEADS, LANES, tb), lambda b: (0, 0, b)),
            pl.BlockSpec((FOX_HEADS, LANES, tb), lambda b: (0, 0, b)),
            pl.BlockSpec((None, pairs, 8, LANES), lambda b: (b, 0, 0, 0)),
        ],
        out_shape=[
            jax.ShapeDtypeStruct((FOX_HEADS, L, LANES), BF16),
            jax.ShapeDtypeStruct((FOX_HEADS, LANES, L), BF16),
            jax.ShapeDtypeStruct((FOX_HEADS, LANES, L), BF16),
            jax.ShapeDtypeStruct((L // tb, pairs, 8, LANES), F32),
        ],
        compiler_params=pltpu.CompilerParams(
            dimension_semantics=("parallel",), vmem_limit_bytes=VMEM_LIMIT),
        name="foxprep",
    )(main, main, main, cum)


def _fox_kernel(tp_ref, ti_ref, tja_ref, tjb_ref, hasb_ref, first_ref, count_ref, cq_ref, ck_ref,
                kaug_a_ref, vaugT_a_ref, kaug_b_ref, vaugT_b_ref, qaugT_ref, o_ref, m_ref, acc_ref, *, tq, tk):
    n = pl.program_id(0)
    p = tp_ref[n]
    i = ti_ref[n]
    live = n < count_ref[0]
    has_b = jnp.logical_and(live, hasb_ref[n] == 1)
    last = ((i + 1) * tq - 1) // tk

    @pl.when(jnp.logical_and(live, first_ref[n] == 1))
    def _():
        m_ref[...] = jnp.full(m_ref.shape, NEG_BIG, F32)
        acc_ref[...] = jnp.zeros(acc_ref.shape, F32)

    def step(j, kaug_ref, vaugT_ref, masked, q_from):
        qs = slice(q_from, tq)
        if masked:
            key = j * tk + lax.broadcasted_iota(jnp.int32, (tk, tq - q_from), 0)
            qry = i * tq + q_from + lax.broadcasted_iota(jnp.int32, (tk, tq - q_from), 1)
            valid = key <= qry
        scores = [jnp.dot(kaug_ref[hh], qaugT_ref[hh, :, qs], preferred_element_type=F32) for hh in range(2)]
        for hh in range(2):
            h = 2 * p + hh
            d = cq_ref[i * FOX_HEADS + h] - ck_ref[j * FOX_HEADS + h]
            sT = scores[hh]
            if masked:
                sT = jnp.where(valid, sT, NEG_BIG)
            m_prev = m_ref[hh, :, qs]
            m_new = jnp.maximum(m_prev, jnp.max(sT, axis=0, keepdims=True) + d)
            alpha = jnp.exp2(m_prev - m_new)
            pT = jnp.exp2((sT - (m_new - d)).astype(BF16))
            acc_ref[hh, :, qs] = alpha * acc_ref[hh, :, qs] + jnp.dot(
                vaugT_ref[hh], pT, preferred_element_type=F32)
            m_ref[hh, :, qs] = m_new

    half = tq // 2
    assert half % LANES == 0

    def block(enabled, j, kaug_ref, vaugT_ref):
        first_seeing = j * tk - i * tq
        crosses_diagonal = (j + 1) * tk - 1 > i * tq
        upper_half_only = first_seeing >= half
        run = lambda masked, q_from: (lambda: step(j, kaug_ref, vaugT_ref, masked, q_from))
        pl.when(enabled & crosses_diagonal & upper_half_only)(run(True, half))
        pl.when(enabled & crosses_diagonal & jnp.logical_not(upper_half_only))(run(True, 0))
        pl.when(enabled & jnp.logical_not(crosses_diagonal))(run(False, 0))

    block(live, tja_ref[n], kaug_a_ref, vaugT_a_ref)
    block(has_b, tjb_ref[n], kaug_b_ref, vaugT_b_ref)

    final_j = jnp.where(has_b, tjb_ref[n], tja_ref[n])

    @pl.when(jnp.logical_and(live, final_j == last))
    def _():
        a0 = acc_ref[0]
        a1 = acc_ref[1]
        o0 = a0[0:FOX_HEAD_DIM, :] / a0[FOX_HEAD_DIM:FOX_HEAD_DIM + 1, :]
        o1 = a1[FOX_HEAD_DIM:, :] / a1[0:1, :]
        o_ref[...] = jnp.concatenate([o0, o1], axis=0).T.astype(BF16)


def _fox_tables(cum, norms, *, tq, tk):
    L = cum.shape[0]
    nq, nk = L // tq, L // tk
    pairs = FOX_HEADS // 2
    assert nq < 256 and nk < 256
    pp, ii, jj = np.meshgrid(np.arange(pairs), np.arange(nq), np.arange(nk), indexing="ij")
    causal = jj * tk <= (ii + 1) * tq - 1
    diagonal = (jj + 1) * tk - 1 > ii * tq
    code = jnp.asarray((pp << 16 | ii << 8 | jj).reshape(-1), jnp.int32)
    n_steps = int(((causal.reshape(pairs * nq, nk).sum(axis=1) + 1) // 2).sum())

    qn = jnp.sqrt(norms[:, :, 0:2, 0]).reshape(nk, FOX_HEADS)
    kn = jnp.sqrt(norms[:, :, 2:4, 0]).reshape(nk, FOX_HEADS)
    qn_i = qn.reshape(nq, tq // tk, FOX_HEADS).max(axis=1)
    kd_i = kn.reshape(nq, tq // tk, FOX_HEADS).max(axis=1)
    cq = cum[::tq] * LOG2E
    ck = cum[::tk] * LOG2E
    c_last = cum[tk - 1::tk] * LOG2E
    bound = qn_i[:, None, :] * (kn[None, :, :] + kd_i[:, None, :]) + (cq[:, None, :] - c_last[None, :, :])
    alive = (bound >= -FOX_SKIP_LOG2).reshape(nq, nk, pairs, 2).any(axis=-1)
    live = (jnp.asarray(causal) & (jnp.asarray(diagonal) | jnp.transpose(alive, (2, 0, 1)))).reshape(-1)
    live2 = live.reshape(pairs * nq, nk).astype(jnp.int32)
    rank = jnp.cumsum(live2, axis=1) - 1
    steps_g = (jnp.sum(live2, axis=1) + 1) // 2
    start_g = jnp.cumsum(steps_g) - steps_g
    step_of = (start_g[:, None] + rank // 2).reshape(-1)
    slot_of = (rank % 2).reshape(-1)
    count = jnp.sum(steps_g)
    step_id = jnp.arange(n_steps, dtype=jnp.int32)
    here = live[None, :] & (step_of[None, :] == step_id[:, None])
    in_a = here & (slot_of[None, :] == 0)
    in_b = here & (slot_of[None, :] == 1)
    code_a = jnp.sum(jnp.where(in_a, code[None, :], 0), axis=1)
    code_b = jnp.sum(jnp.where(in_b, code[None, :], 0), axis=1)
    active = step_id < count
    has_b = jnp.any(in_b, axis=1) & active
    code_a = jnp.where(active, code_a, jnp.max(jnp.where(live & (slot_of == 0), code, -1)))
    code_b = jnp.where(has_b, code_b, code_a)
    tp, ti, tja, tjb = code_a >> 16, (code_a >> 8) & 255, code_a & 255, code_b & 255
    first = jnp.concatenate([jnp.ones((1,), jnp.int32),
                             ((tp[1:] != tp[:-1]) | (ti[1:] != ti[:-1])).astype(jnp.int32)])
    return (tp, ti, tja, tjb, has_b.astype(jnp.int32), first, count.reshape(1).astype(jnp.int32),
            cq.reshape(-1), ck.reshape(-1))


def _fox(main, cum, *, tq, tk):
    L = main.shape[0]
    kaug, qaugT, vaugT, norms = _foxprep(main, cum, tb=tk)
    tables = _fox_tables(cum, norms, tq=tq, tk=tk)
    grid_spec = pltpu.PrefetchScalarGridSpec(
        num_scalar_prefetch=len(tables),
        grid=(tables[0].shape[0],),
        in_specs=[
            pl.BlockSpec((2, tk, LANES), lambda n, tp, ti, tja, tjb, *_: (tp[n], tja[n], 0)),
            pl.BlockSpec((2, LANES, tk), lambda n, tp, ti, tja, tjb, *_: (tp[n], 0, tja[n])),
            pl.BlockSpec((2, tk, LANES), lambda n, tp, ti, tja, tjb, *_: (tp[n], tjb[n], 0)),
            pl.BlockSpec((2, LANES, tk), lambda n, tp, ti, tja, tjb, *_: (tp[n], 0, tjb[n])),
            pl.BlockSpec((2, LANES, tq), lambda n, tp, ti, *_: (tp[n], 0, ti[n])),
        ],
        out_specs=pl.BlockSpec((tq, LANES), lambda n, tp, ti, *_: (ti[n], tp[n])),
        scratch_shapes=[
            pltpu.VMEM((2, 1, tq), F32),
            pltpu.VMEM((2, LANES, tq), F32),
        ],
    )
    return pl.pallas_call(
        functools.partial(_fox_kernel, tq=tq, tk=tk),
        grid_spec=grid_spec,
        out_shape=jax.ShapeDtypeStruct((L, FOX_INNER), BF16),
        compiler_params=pltpu.CompilerParams(
            dimension_semantics=("arbitrary",), vmem_limit_bytes=VMEM_LIMIT),
        name="fox",
    )(*tables, kaug, vaugT, kaug, vaugT, qaugT)


def _mid_kernel(x_ref, ys_ref, yf_ref, wo_ref, ln2_ref, wq_ref, k1_ref, k2_ref,
                x1_ref, h2T_ref, s1T_ref, s2T_ref):
    mix = jnp.dot(ys_ref[...], wo_ref[0:SSD_INNER, :], preferred_element_type=F32)
    mix = mix + jnp.dot(yf_ref[...], wo_ref[SSD_INNER:, :], preferred_element_type=F32)
    x1 = x_ref[...] + mix
    x1_ref[...] = x1
    h2 = x1 * lax.rsqrt(jnp.mean(x1 * x1, axis=-1, keepdims=True) + EPS) * ln2_ref[...]
    h2T_ref[...] = h2.T.astype(BF16)
    qp = jnp.dot(h2.astype(BF16), wq_ref[...], preferred_element_type=F32)
    half = PEER_KEY_DIM // 2
    for h in range(PEER_HEADS):
        q1 = qp[:, h * PEER_KEY_DIM:h * PEER_KEY_DIM + half].astype(BF16)
        q2 = qp[:, h * PEER_KEY_DIM + half:(h + 1) * PEER_KEY_DIM].astype(BF16)
        s1T_ref[h] = lax.dot_general(k1_ref[h], q1, _NT, preferred_element_type=F32)
        s2T_ref[h] = lax.dot_general(k2_ref[h], q2, _NT, preferred_element_type=F32)


def _mid(x2d, y_ssd, y_fox, w_out, ln2_w, wq, k1, k2, *, tm):
    L = x2d.shape[0]
    half = PEER_KEY_DIM // 2
    resident = lambda shape: pl.BlockSpec(shape, lambda i: (0,) * len(shape), pipeline_mode=pl.Buffered(1))
    return pl.pallas_call(
        _mid_kernel,
        grid=(L // tm,),
        in_specs=[
            pl.BlockSpec((tm, D_MODEL), lambda i: (i, 0)),
            pl.BlockSpec((tm, SSD_INNER), lambda i: (i, 0)),
            pl.BlockSpec((tm, FOX_INNER), lambda i: (i, 0)),
            resident((D_MODEL, D_MODEL)),
            resident((1, D_MODEL)),
            resident((D_MODEL, PEER_HEADS * PEER_KEY_DIM)),
            resident((PEER_HEADS, PEER_N_KEYS, half)),
            resident((PEER_HEADS, PEER_N_KEYS, half)),
        ],
        out_specs=[
            pl.BlockSpec((tm, D_MODEL), lambda i: (i, 0)),
            pl.BlockSpec((D_MODEL, tm), lambda i: (0, i)),
            pl.BlockSpec((PEER_HEADS, PEER_N_KEYS, tm), lambda i: (0, 0, i)),
            pl.BlockSpec((PEER_HEADS, PEER_N_KEYS, tm), lambda i: (0, 0, i)),
        ],
        out_shape=[
            jax.ShapeDtypeStruct((L, D_MODEL), F32),
            jax.ShapeDtypeStruct((D_MODEL, L), BF16),
            jax.ShapeDtypeStruct((PEER_HEADS, PEER_N_KEYS, L), F32),
            jax.ShapeDtypeStruct((PEER_HEADS, PEER_N_KEYS, L), F32),
        ],
        compiler_params=pltpu.CompilerParams(
            dimension_semantics=("parallel",), vmem_limit_bytes=VMEM_LIMIT),
        name="mid",
    )(x2d, y_ssd, y_fox, w_out, ln2_w.reshape(1, D_MODEL), wq, k1, k2)


def _top16(s, order, exact):
    axis = s.ndim - 2
    slot = lax.broadcasted_iota(jnp.int32, s.shape[:-2] + (PEER_TOPK, s.shape[-1]), axis)
    unranked = float(PEER_TOPK)

    if not exact:
        def next_value(r, carry):
            below, vals = carry
            m = jnp.max(jnp.where(s < below, s, -jnp.inf), axis=axis, keepdims=True)
            return m, jnp.where(slot == r, m, vals)

        ceiling = jnp.full(s.shape[:-2] + (1, s.shape[-1]), jnp.inf, F32)
        lowest, vals = lax.fori_loop(0, PEER_TOPK, next_value, (ceiling, jnp.zeros(slot.shape, F32)))
        taken = s >= lowest
        return None, vals, taken, jnp.sum(jnp.where(taken, 1.0, 0.0), axis=axis, keepdims=True)

    def extract(r, carry):
        s, rank, vals = carry
        m = jnp.max(s, axis=axis, keepdims=True)
        first = jnp.min(jnp.where(s == m, order, jnp.inf), axis=axis, keepdims=True)
        hit = order == first
        rank = jnp.where(hit, lax.convert_element_type(r, F32), rank)
        s = jnp.where(hit, -jnp.inf, s)
        return s, rank, jnp.where(slot == r, m, vals)

    init = (s, jnp.full(s.shape, unranked, F32), jnp.zeros(slot.shape, F32))
    _, rank, vals = lax.fori_loop(0, PEER_TOPK, extract, init)
    taken = rank < unranked
    return rank, vals, taken, jnp.sum(jnp.where(taken, 1.0, 0.0), axis=axis, keepdims=True)


def _topk_kernel(s1T_ref, s2T_ref, lim_ref, e1_ref, r2_ref, e2_ref):
    t = s1T_ref.shape[2]
    key_id = lax.broadcasted_iota(jnp.int32, (PEER_N_KEYS, t), 0).astype(F32)
    half_k = PEER_TOPK // 2
    pos_main = lax.broadcasted_iota(jnp.int32, (half_k * PEER_TOPK, t), 0)
    pos_tail = (lax.broadcasted_iota(jnp.int32, (half_k, t), 0) + half_k) * PEER_TOPK
    pos = jnp.concatenate([pos_main, pos_tail], axis=0).astype(F32)
    slot = lax.broadcasted_iota(jnp.int32, (PEER_TOPK, t), 0)

    def solve(h, exact):
        s1 = s1T_ref[h]
        s2 = s2T_ref[h]
        rank12, v12, _, n12 = _top16(jnp.stack([s1, s2]), key_id, exact)
        v1, v2 = v12[0], v12[1]

        cand = jnp.concatenate(
            [v1[a:a + 1, :] + v2 for a in range(half_k)] + [v1[half_k:, :] + v2[0:1, :]], axis=0)
        _, top_sums, taken_c, n_c = _top16(cand, pos, exact)
        sel = jnp.where(taken_c, 1.0, 0.0)
        z = jnp.sum(jnp.exp(top_sums - top_sums[0:1, :]), axis=0, keepdims=True)
        n_taken = jnp.maximum(jnp.maximum(n12[0], n12[1]), n_c)

        if exact:
            rank1, rank2 = rank12[0], rank12[1]
            is_rank1 = lambda a: rank1 == float(a)
        else:
            rank2 = jnp.full(s2.shape, float(PEER_TOPK), F32)
            for a in range(PEER_TOPK):
                rank2 = jnp.where(s2 == v2[a:a + 1, :], float(a), rank2)
            is_rank1 = lambda a: s1 == v1[a:a + 1, :]

        bcount = jnp.zeros((PEER_TOPK, t), F32)
        for a in range(half_k):
            cnt = jnp.sum(sel[a * PEER_TOPK:(a + 1) * PEER_TOPK, :], axis=0, keepdims=True)
            bcount = jnp.where(slot == a, cnt, bcount)
        tail = jnp.concatenate([jnp.zeros((half_k, t), F32), sel[half_k * PEER_TOPK:, :]], axis=0)
        bcount = jnp.where(slot >= half_k, tail, bcount)

        lim = jnp.zeros((PEER_N_KEYS, t), F32)
        for a in range(PEER_TOPK):
            lim = jnp.where(is_rank1(a), bcount[a:a + 1, :], lim)

        lim_ref[h] = lim
        e1_ref[h] = jnp.exp(s1 - v1[0:1, :])
        r2_ref[h] = rank2.astype(BF16)
        e2_ref[h] = (jnp.exp(s2 - v2[0:1, :]) / z).astype(BF16)
        return jnp.max(n_taken)

    def head(h, _):
        most = solve(h, exact=False)

        @pl.when(most > float(PEER_TOPK))
        def _():
            solve(h, exact=True)

        return 0

    lax.fori_loop(0, PEER_HEADS, head, 0)


def _topk(s1T, s2T, *, tt):
    L = s1T.shape[2]
    spec = pl.BlockSpec((PEER_HEADS, PEER_N_KEYS, tt), lambda i: (0, 0, i))
    shape = jax.ShapeDtypeStruct((PEER_HEADS, PEER_N_KEYS, L), F32)
    shape_b = jax.ShapeDtypeStruct((PEER_HEADS, PEER_N_KEYS, L), BF16)
    return pl.pallas_call(
        _topk_kernel,
        grid=(L // tt,),
        in_specs=[spec, spec],
        out_specs=[spec] * 4,
        out_shape=[shape, shape, shape_b, shape_b],
        compiler_params=pltpu.CompilerParams(
            dimension_semantics=("parallel",), vmem_limit_bytes=VMEM_LIMIT),
        name="topk",
    )(s1T, s2T)


def _peer_kernel(h2T_ref, u_ref, unext_ref, vT_ref, lim_ref, e1_ref, r2_ref, e2_ref, x1_ref, lnf_ref,
                 o_ref, acc_ref, act0_ref, *, et, sub):
    e = pl.program_id(1)
    n_sub = et // sub
    keys_per_sub = sub // PEER_N_KEYS

    def activation(rows_ref, s):
        return jnp.dot(rows_ref[s * sub:(s + 1) * sub, :], h2T_ref[...], preferred_element_type=F32)

    @pl.when(e == 0)
    def _():
        acc_ref[...] = jnp.zeros(acc_ref.shape, F32)
        act0_ref[...] = activation(u_ref, 0)

    def gate(s):
        parts = []
        for i in range(keys_per_sub):
            i1 = e * (et // PEER_N_KEYS) + s * keys_per_sub + i
            w = None
            for h in range(PEER_HEADS):
                lim_row = lim_ref[h, pl.ds(i1, 1), :].astype(BF16)
                e1_row = e1_ref[h, pl.ds(i1, 1), :].astype(BF16)
                term = jnp.where(r2_ref[h] < lim_row, e2_ref[h] * e1_row, jnp.zeros((), BF16))
                w = term if w is None else w + term
            parts.append(w)
        return jnp.concatenate(parts, axis=0)

    total = None
    a = act0_ref[...]
    for s in range(n_sub):
        a_next = activation(u_ref, s + 1) if s + 1 < n_sub else activation(unext_ref, 0)
        gelu = 0.5 * a * (1.0 + lax.erf(a * (2.0 ** -0.5)))
        wg = gelu.astype(BF16) * gate(s)
        part = jnp.dot(vT_ref[:, s * sub:(s + 1) * sub], wg, preferred_element_type=F32)
        total = part if total is None else total + part
        a = a_next
    act0_ref[...] = a
    acc_ref[...] += total

    @pl.when(e == pl.num_programs(1) - 1)
    def _():
        x2 = x1_ref[...] + acc_ref[...].T
        o_ref[...] = x2 * lax.rsqrt(jnp.mean(x2 * x2, axis=-1, keepdims=True) + EPS) * lnf_ref[...]


def _peer(h2T, u_b, vT_b, lim, e1, r2, e2, x1, lnf_w, *, tt, et, sub):
    L = x1.shape[0]
    tok = pl.BlockSpec((PEER_HEADS, PEER_N_KEYS, tt), lambda t, e: (0, 0, t), pipeline_mode=pl.Buffered(1))
    n_blocks = PEER_N_EXPERTS // et
    return pl.pallas_call(
        functools.partial(_peer_kernel, et=et, sub=sub),
        grid=(L // tt, n_blocks),
        in_specs=[
            pl.BlockSpec((D_MODEL, tt), lambda t, e: (0, t), pipeline_mode=pl.Buffered(1)),
            pl.BlockSpec((et, D_MODEL), lambda t, e: (e, 0)),
            pl.BlockSpec((sub, D_MODEL), lambda t, e: (jnp.minimum(e + 1, n_blocks - 1) * (et // sub), 0)),
            pl.BlockSpec((D_MODEL, et), lambda t, e: (0, e)),
            tok, tok, tok, tok,
            pl.BlockSpec((tt, D_MODEL), lambda t, e: (t, 0), pipeline_mode=pl.Buffered(1)),
            pl.BlockSpec((1, D_MODEL), lambda t, e: (0, 0)),
        ],
        out_specs=pl.BlockSpec((tt, D_MODEL), lambda t, e: (t, 0)),
        out_shape=jax.ShapeDtypeStruct((L, D_MODEL), F32),
        scratch_shapes=[
            pltpu.VMEM((D_MODEL, tt), F32),
            pltpu.VMEM((sub, tt), F32),
        ],
        compiler_params=pltpu.CompilerParams(
            dimension_semantics=("parallel", "arbitrary"), vmem_limit_bytes=VMEM_LIMIT),
        name="peer",
    )(h2T, u_b, u_b, vT_b, lim, e1, r2, e2, x1, lnf_w.reshape(1, D_MODEL))


def _tiles(L):
    return dict(
        inproj_tm=min(512, L), inproj_tn=2816,
        ssd_q=min(128, L),
        fox_tq=min(1024, L), fox_tk=min(512, L),
        mid_tm=min(256, L),
        topk_tt=min(256, L),
        peer_tt=min(512, L), peer_et=1024, peer_sub=256,
    )


def kernel(x, ln1_w, w_in, conv_w, conv_b, dt_bias, a_log, d_skip, ssd_norm_w, fox_f_bias,
           w_out, ln2_w, peer_wq, peer_k1, peer_k2, peer_u, peer_v, lnf_w):
    B, L, D = x.shape
    assert B == 1 and D == D_MODEL
    t = _tiles(L)
    x2d = x.reshape(L, D)

    c_z, c_xbc, c_dt, c_q = 0, SSD_INNER, 2560, 2576
    c_f = c_q + 3 * FOX_INNER
    w_main = jnp.concatenate([w_in[:, c_q:c_f], w_in[:, c_z:c_dt]], axis=1).astype(BF16)
    w_small = jnp.concatenate(
        [w_in[:, c_dt:c_q], w_in[:, c_f:], jnp.zeros((D, SMALL_DIM - SSD_HEADS - FOX_HEADS), F32)],
        axis=1).astype(BF16)

    main, small, smallT = _inproj(x2d, ln1_w, w_main, w_small, tm=t["inproj_tm"], tn=t["inproj_tn"])
    y_ssd, cum = _ssd(main, small, smallT, conv_w, conv_b, dt_bias, a_log, d_skip, ssd_norm_w,
                      fox_f_bias, q=t["ssd_q"])
    y_fox = _fox(main, cum, tq=t["fox_tq"], tk=t["fox_tk"])
    x1, h2T, s1T, s2T = _mid(x2d, y_ssd, y_fox, w_out.astype(BF16), ln2_w, peer_wq.astype(BF16),
                             peer_k1.astype(BF16), peer_k2.astype(BF16), tm=t["mid_tm"])
    lim, e1, r2, e2 = _topk(s1T, s2T, tt=t["topk_tt"])
    out = _peer(h2T, peer_u.astype(BF16), peer_v.T.astype(BF16), lim, e1, r2, e2, x1, lnf_w,
                tt=t["peer_tt"], et=t["peer_et"], sub=t["peer_sub"])
    return out.reshape(B, L, D)
```

```python
import functools

import numpy as np
import jax
import jax.numpy as jnp
from jax import lax
from jax.experimental import pallas as pl
from jax.experimental.pallas import tpu as pltpu

F32 = jnp.float32
BF16 = jnp.bfloat16

D_MODEL = 2048
EPS = 1e-6

SSD_HEAD_DIM = 64
SSD_INNER = 1024
SSD_HEADS = 16
SSD_GROUPS = 2
SSD_STATE = 128
SSD_CONV = 4
SSD_TAIL = 16
SSD_BC = 2 * SSD_GROUPS * SSD_STATE
SSD_GROUP_WIDTH = SSD_INNER // SSD_GROUPS

FOX_HEAD_DIM = 64
FOX_INNER = 1024
FOX_HEADS = 16

PEER_HEADS = 8
PEER_N_KEYS = 128
PEER_N_EXPERTS = PEER_N_KEYS * PEER_N_KEYS
PEER_KEY_DIM = 256
PEER_TOPK = 16

COL_Q = 0
COL_K = 1024
COL_V = 2048
COL_Z = 3072
COL_XS = 4096
COL_BC = 5120
MAIN_DIM = 5632
SMALL_DIM = 128

LANES = 128
VMEM_LIMIT = 56 * 1024 * 1024

NEG_BIG = -1e30
LOG2E = 1.4426950408889634

_NT = (((1,), (1,)), ((), ()))
_TN = (((0,), (0,)), ((), ()))


def _softplus(x):
    return jnp.maximum(x, 0.0) + jnp.log1p(jnp.exp(-jnp.abs(x)))


def _silu(x):
    return x * (0.5 * jnp.tanh(0.5 * x) + 0.5)


def _split3(x):
    hi = x.astype(BF16)
    r = x - hi.astype(F32)
    mid = r.astype(BF16)
    lo = (r - mid.astype(F32)).astype(BF16)
    return hi, mid, lo


def _dot3_lhs(x, w):
    a, b, c = _split3(x)
    d = lambda p: jnp.dot(p, w, preferred_element_type=F32)
    return d(a) + d(b) + d(c)


def _dot3_rhs(w, x):
    a, b, c = _split3(x)
    d = lambda p: jnp.dot(w, p, preferred_element_type=F32)
    return d(a) + d(b) + d(c)


def _inproj_kernel(x_ref, lnw_ref, wm_ref, ws_ref, wsT_ref, main_ref, small_ref, smallT_ref, h_ref):
    @pl.when(pl.program_id(1) == 0)
    def _():
        x = x_ref[...]
        h = x * lax.rsqrt(jnp.mean(x * x, axis=-1, keepdims=True) + EPS) * lnw_ref[...]
        hb = h.astype(BF16)
        h_ref[...] = hb
        small_ref[...] = jnp.dot(hb, ws_ref[...], preferred_element_type=F32)
        smallT_ref[...] = lax.dot_general(wsT_ref[...], hb, _NT, preferred_element_type=F32)

    main_ref[...] = jnp.dot(h_ref[...], wm_ref[...], preferred_element_type=F32).astype(BF16)


def _inproj(x2d, ln1_w, w_main, w_small, *, tm, tn):
    L = x2d.shape[0]
    return pl.pallas_call(
        _inproj_kernel,
        grid=(L // tm, MAIN_DIM // tn),
        in_specs=[
            pl.BlockSpec((tm, D_MODEL), lambda i, j: (i, 0)),
            pl.BlockSpec((1, D_MODEL), lambda i, j: (0, 0)),
            pl.BlockSpec((D_MODEL, tn), lambda i, j: (0, j)),
            pl.BlockSpec((D_MODEL, SMALL_DIM), lambda i, j: (0, 0)),
            pl.BlockSpec((SMALL_DIM, D_MODEL), lambda i, j: (0, 0)),
        ],
        out_specs=[
            pl.BlockSpec((tm, tn), lambda i, j: (i, j)),
            pl.BlockSpec((tm, SMALL_DIM), lambda i, j: (i, 0)),
            pl.BlockSpec((SMALL_DIM, tm), lambda i, j: (0, i)),
        ],
        out_shape=[
            jax.ShapeDtypeStruct((L, MAIN_DIM), BF16),
            jax.ShapeDtypeStruct((L, SMALL_DIM), F32),
            jax.ShapeDtypeStruct((SMALL_DIM, L), F32),
        ],
        scratch_shapes=[pltpu.VMEM((tm, D_MODEL), BF16)],
        compiler_params=pltpu.CompilerParams(
            dimension_semantics=("parallel", "arbitrary"), vmem_limit_bytes=VMEM_LIMIT),
        name="inproj",
    )(x2d, ln1_w.reshape(1, D_MODEL), w_main, w_small, w_small.T)


def _ssd_kernel(z_ref, xs_ref, bc_ref, small_ref, smallT_ref,
                cwx_ref, cbx_ref, cwbc_ref, cbbc_ref,
                dtb_ref, dtbT_ref, alog_ref, alogT_ref, fb_ref,
                dfull_ref, normw_ref, expand_ref,
                y_ref, cum_ref,
                extx_ref, extbc_ref, state_ref, fcarry_ref, *, q):
    c = pl.program_id(0)
    tail = SSD_TAIL

    @pl.when(c == 0)
    def _():
        extx_ref[0:tail, :] = jnp.zeros((tail, SSD_INNER), F32)
        extbc_ref[0:tail, :] = jnp.zeros((tail, SSD_BC), F32)
        state_ref[...] = jnp.zeros(state_ref.shape, F32)
        fcarry_ref[...] = jnp.zeros(fcarry_ref.shape, F32)

    extx_ref[tail:tail + q, :] = xs_ref[...].astype(F32)
    extbc_ref[tail:tail + q, :] = bc_ref[...].astype(F32)

    srow = lax.broadcasted_iota(jnp.int32, (q, q + tail), 0)
    scol = lax.broadcasted_iota(jnp.int32, (q, q + tail), 1)

    def conv(ext_ref, w_ref, b_ref):
        acc = b_ref[...] + ext_ref[tail:tail + q, :] * w_ref[SSD_CONV - 1:SSD_CONV, :]
        ext = ext_ref[...].astype(BF16)
        for k in range(SSD_CONV - 1):
            off = tail - (SSD_CONV - 1) + k
            shift = jnp.where(scol == srow + off, 1.0, 0.0).astype(BF16)
            acc = acc + jnp.dot(shift, ext, preferred_element_type=F32) * w_ref[k:k + 1, :]
        return acc

    xs = _silu(conv(extx_ref, cwx_ref, cbx_ref))
    bc = _silu(conv(extbc_ref, cwbc_ref, cbbc_ref))
    extx_ref[0:tail, :] = extx_ref[q:q + tail, :]
    extbc_ref[0:tail, :] = extbc_ref[q:q + tail, :]

    row = lax.broadcasted_iota(jnp.int32, (q, q), 0)
    col = lax.broadcasted_iota(jnp.int32, (q, q), 1)
    lower = row >= col
    tri = lower.astype(BF16)
    upper = (row <= col).astype(BF16)

    small = small_ref[...]
    smallT = smallT_ref[...]
    neg_a = -jnp.exp(alog_ref[...])
    neg_aT = -jnp.exp(alogT_ref[...])
    dt = _softplus(small[:, 0:SSD_HEADS] + dtb_ref[...])
    dtT = _softplus(smallT[0:SSD_HEADS, :] + dtbT_ref[...])
    a_cum = _dot3_rhs(tri, dt * neg_a)
    a_cumT = _dot3_lhs(dtT * neg_aT, upper)
    a_last = a_cum[q - 1:q, :]

    logf = -_softplus(-(small[:, SSD_HEADS:SSD_HEADS + FOX_HEADS] + fb_ref[...]))
    cumf = fcarry_ref[...] + _dot3_rhs(tri, logf)
    cum_ref[...] = cumf
    fcarry_ref[...] = cumf[q - 1:q, :]

    expand = expand_ref[...]
    dt_full = _dot3_lhs(dt, expand)
    ea_full = _dot3_lhs(jnp.exp(a_cum), expand)
    ds_full = _dot3_lhs(jnp.exp(a_last - a_cum), expand)

    xdt = xs * dt_full
    xdt_b = xdt.astype(BF16)
    xw_b = (xdt * ds_full).astype(BF16)
    bc_b = bc.astype(BF16)
    lane = lax.broadcasted_iota(jnp.int32, (q, LANES), 1)

    pieces = []
    for g in range(SSD_GROUPS):
        b_g = bc_b[:, g * SSD_STATE:(g + 1) * SSD_STATE]
        c_g = bc_b[:, (SSD_GROUPS + g) * SSD_STATE:(SSD_GROUPS + g + 1) * SSD_STATE]
        cb = lax.dot_general(c_g, b_g, _NT, preferred_element_type=F32)
        heads_per_group = SSD_HEADS // SSD_GROUPS
        for pp in range(heads_per_group // 2):
            pair = g * (heads_per_group // 2) + pp
            xp = xdt_b[:, pair * LANES:(pair + 1) * LANES]
            ys = []
            for hh in range(2):
                h = 2 * pair + hh
                seg = a_cum[:, h:h + 1] - a_cumT[h:h + 1, :]
                decay = jnp.exp(jnp.where(lower, seg, -jnp.inf))
                m = (cb * decay).astype(BF16)
                ys.append(jnp.dot(m, xp, preferred_element_type=F32))
            pieces.append(jnp.where(lane < SSD_HEAD_DIM, ys[0], ys[1]))
    y_diag = jnp.concatenate(pieces, axis=1)

    offs = []
    for g in range(SSD_GROUPS):
        b_g = bc_b[:, g * SSD_STATE:(g + 1) * SSD_STATE]
        c_g = bc_b[:, (SSD_GROUPS + g) * SSD_STATE:(SSD_GROUPS + g + 1) * SSD_STATE]
        gs = slice(g * SSD_GROUP_WIDTH, (g + 1) * SSD_GROUP_WIDTH)
        st = state_ref[g]
        offs.append(jnp.dot(c_g, st.astype(BF16), preferred_element_type=F32))
        state_ref[g] = st * ea_full[q - 1:q, gs] + lax.dot_general(
            b_g, xw_b[:, gs], _TN, preferred_element_type=F32)
    y_off = jnp.concatenate(offs, axis=1) * ea_full

    y = y_diag + y_off + dfull_ref[...] * xs
    y = y * _silu(z_ref[...].astype(F32))
    normed = []
    for g in range(SSD_GROUPS):
        yg = y[:, g * SSD_GROUP_WIDTH:(g + 1) * SSD_GROUP_WIDTH]
        normed.append(yg * lax.rsqrt(jnp.mean(yg * yg, axis=-1, keepdims=True) + EPS))
    y_ref[...] = (jnp.concatenate(normed, axis=1) * normw_ref[...]).astype(BF16)


def _ssd(main, small, smallT, conv_w, conv_b, dt_bias, a_log, d_skip, ssd_norm_w, fox_f_bias, *, q):
    L = main.shape[0]
    expand = (jnp.arange(SSD_INNER)[None, :] // SSD_HEAD_DIM == jnp.arange(SSD_HEADS)[:, None]).astype(BF16)
    const = lambda shape: pl.BlockSpec(shape, lambda c: (0,) * len(shape))
    return pl.pallas_call(
        functools.partial(_ssd_kernel, q=q),
        grid=(L // q,),
        in_specs=[
            pl.BlockSpec((q, SSD_INNER), lambda c: (c, COL_Z // SSD_INNER)),
            pl.BlockSpec((q, SSD_INNER), lambda c: (c, COL_XS // SSD_INNER)),
            pl.BlockSpec((q, SSD_BC), lambda c: (c, COL_BC // SSD_BC)),
            pl.BlockSpec((q, SMALL_DIM), lambda c: (c, 0)),
            pl.BlockSpec((SMALL_DIM, q), lambda c: (0, c)),
            const((SSD_CONV, SSD_INNER)), const((1, SSD_INNER)),
            const((SSD_CONV, SSD_BC)), const((1, SSD_BC)),
            const((1, SSD_HEADS)), const((SSD_HEADS, 1)),
            const((1, SSD_HEADS)), const((SSD_HEADS, 1)),
            const((1, FOX_HEADS)),
            const((1, SSD_INNER)), const((1, SSD_INNER)),
            const((SSD_HEADS, SSD_INNER)),
        ],
        out_specs=[
            pl.BlockSpec((q, SSD_INNER), lambda c: (c, 0)),
            pl.BlockSpec((q, FOX_HEADS), lambda c: (c, 0)),
        ],
        out_shape=[
            jax.ShapeDtypeStruct((L, SSD_INNER), BF16),
            jax.ShapeDtypeStruct((L, FOX_HEADS), F32),
        ],
        scratch_shapes=[
            pltpu.VMEM((q + SSD_TAIL, SSD_INNER), F32),
            pltpu.VMEM((q + SSD_TAIL, SSD_BC), F32),
            pltpu.VMEM((SSD_GROUPS, SSD_STATE, SSD_GROUP_WIDTH), F32),
            pltpu.VMEM((1, FOX_HEADS), F32),
        ],
        compiler_params=pltpu.CompilerParams(
            dimension_semantics=("arbitrary",), vmem_limit_bytes=VMEM_LIMIT),
        name="ssd",
    )(main, main, main, small, smallT,
      conv_w[:, :SSD_INNER], conv_b[:SSD_INNER].reshape(1, -1),
      conv_w[:, SSD_INNER:], conv_b[SSD_INNER:].reshape(1, -1),
      dt_bias.reshape(1, -1), dt_bias.reshape(-1, 1),
      a_log.reshape(1, -1), a_log.reshape(-1, 1),
      fox_f_bias.reshape(1, -1),
      jnp.repeat(d_skip, SSD_HEAD_DIM).reshape(1, -1), ssd_norm_w.reshape(1, -1),
      expand)


FOX_AUG = 3
FOX_SKIP_LOG2 = 160.0


def _foxprep_kernel(q_ref, k_ref, v_ref, cum_ref, kaug_ref, qaugT_ref, vaugT_ref, norm_ref):
    stat_row = lax.broadcasted_iota(jnp.int32, (8, LANES), 0)
    tb = q_ref.shape[0]
    lane = lax.broadcasted_iota(jnp.int32, (tb, LANES), 1)
    cum = cum_ref[...]
    rel = (cum[0:1, :] - cum) * LOG2E
    hrow = lax.broadcasted_iota(jnp.int32, (FOX_HEADS, FOX_INNER), 0)
    hcol = lax.broadcasted_iota(jnp.int32, (FOX_HEADS, FOX_INNER), 1)
    spare0 = (hrow // 2) * LANES + jnp.where(hrow % 2 == 0, FOX_HEAD_DIM, 0)
    aug_all = None
    for r, piece in enumerate(_split3(rel)):
        place = jnp.where(hcol == spare0 + r, 1.0, 0.0).astype(BF16)
        term = jnp.dot(piece, place, preferred_element_type=F32)
        aug_all = term if aug_all is None else aug_all + term
    aug_all = aug_all.astype(BF16)
    eye = jnp.where(lax.broadcasted_iota(jnp.int32, (LANES, LANES), 0)
                    == lax.broadcasted_iota(jnp.int32, (LANES, LANES), 1), 1.0, 0.0).astype(BF16)
    transpose = lambda xb: lax.dot_general(eye, xb, _NT, preferred_element_type=F32)
    trow = lax.broadcasted_iota(jnp.int32, (LANES, tb), 0)
    for p in range(FOX_HEADS // 2):
        cols = slice(p * LANES, (p + 1) * LANES)
        q = q_ref[:, cols].astype(F32) * (FOX_HEAD_DIM ** -0.5 * LOG2E)
        k = k_ref[:, cols]
        v = v_ref[:, cols].astype(F32)
        kT = transpose(k)
        stats = jnp.zeros((8, LANES), F32)
        for hh in range(2):
            head_lanes = (lane < FOX_HEAD_DIM) if hh == 0 else (lane >= FOX_HEAD_DIM)
            base = FOX_HEAD_DIM if hh == 0 else 0
            h = 2 * p + hh
            kaug_ref[h] = jnp.where(head_lanes, k, aug_all[:, cols])
            is_aug = (lane >= base) & (lane < base + FOX_AUG)
            qa = jnp.where(head_lanes, q, jnp.where(is_aug, 1.0, 0.0)).astype(BF16)
            va = jnp.where(head_lanes, v, jnp.where(lane == base, 1.0, 0.0)).astype(BF16)
            qaT = transpose(qa)
            qaugT_ref[h] = qaT.astype(BF16)
            vaugT_ref[h] = transpose(va).astype(BF16)
            head_rows = (trow < FOX_HEAD_DIM) if hh == 0 else (trow >= FOX_HEAD_DIM)
            sq_norms = lambda xT: jnp.sum(jnp.where(head_rows, xT * xT, 0.0), axis=0, keepdims=True)
            qn2 = jnp.max(sq_norms(qaT), axis=1, keepdims=True)
            kn2 = jnp.max(sq_norms(kT), axis=1, keepdims=True)
            stats = jnp.where(stat_row == hh, qn2, jnp.where(stat_row == 2 + hh, kn2, stats))
        norm_ref[p] = stats


def _foxprep(main, cum, *, tb):
    L = main.shape[0]
    pairs = FOX_HEADS // 2
    return pl.pallas_call(
        _foxprep_kernel,
        grid=(L // tb,),
        in_specs=[
            pl.BlockSpec((tb, FOX_INNER), lambda b: (b, COL_Q // FOX_INNER)),
            pl.BlockSpec((tb, FOX_INNER), lambda b: (b, COL_K // FOX_INNER)),
            pl.BlockSpec((tb, FOX_INNER), lambda b: (b, COL_V // FOX_INNER)),
            pl.BlockSpec((tb, FOX_HEADS), lambda b: (b, 0)),
        ],
        out_specs=[
            pl.BlockSpec((FOX_HEADS, tb, LANES), lambda b: (0, b, 0)),
            pl.BlockSpec((FOX_HEADS, LANES, tb), lambda b: (0, 0, b)),
            pl.BlockSpec((FOX_HEADS, LANES, tb), lambda b: (0, 0, b)),
            pl.BlockSpec((None, pairs, 8, LANES), lambda b: (b, 0, 0, 0)),
        ],
        out_shape=[
            jax.ShapeDtypeStruct((FOX_HEADS, L, LANES), BF16),
            jax.ShapeDtypeStruct((FOX_HEADS, LANES, L), BF16),
            jax.ShapeDtypeStruct((FOX_HEADS, LANES, L), BF16),
            jax.ShapeDtypeStruct((L // tb, pairs, 8, LANES), F32),
        ],
        compiler_params=pltpu.CompilerParams(
            dimension_semantics=("parallel",), vmem_limit_bytes=VMEM_LIMIT),
        name="foxprep",
    )(main, main, main, cum)


def _fox_kernel(tp_ref, ti_ref, tja_ref, tjb_ref, hasb_ref, first_ref, count_ref, cq_ref, ck_ref,
                kaug_a_ref, vaugT_a_ref, kaug_b_ref, vaugT_b_ref, qaugT_ref, o_ref, m_ref, acc_ref, *, tq, tk):
    n = pl.program_id(0)
    p = tp_ref[n]
    i = ti_ref[n]
    live = n < count_ref[0]
    has_b = jnp.logical_and(live, hasb_ref[n] == 1)
    last = ((i + 1) * tq - 1) // tk

    @pl.when(jnp.logical_and(live, first_ref[n] == 1))
    def _():
        m_ref[...] = jnp.full(m_ref.shape, NEG_BIG, F32)
        acc_ref[...] = jnp.zeros(acc_ref.shape, F32)

    def step(j, kaug_ref, vaugT_ref, masked, q_from):
        qs = slice(q_from, tq)
        if masked:
            key = j * tk + lax.broadcasted_iota(jnp.int32, (tk, tq - q_from), 0)
            qry = i * tq + q_from + lax.broadcasted_iota(jnp.int32, (tk, tq - q_from), 1)
            valid = key <= qry
        scores = [jnp.dot(kaug_ref[hh], qaugT_ref[hh, :, qs], preferred_element_type=F32) for hh in range(2)]
        for hh in range(2):
            h = 2 * p + hh
            d = cq_ref[i * FOX_HEADS + h] - ck_ref[j * FOX_HEADS + h]
            sT = scores[hh]
            if masked:
                sT = jnp.where(valid, sT, NEG_BIG)
            m_prev = m_ref[hh, :, qs]
            m_new = jnp.maximum(m_prev, jnp.max(sT, axis=0, keepdims=True) + d)
            alpha = jnp.exp2(m_prev - m_new)
            pT = jnp.exp2((sT - (m_new - d)).astype(BF16))
            acc_ref[hh, :, qs] = alpha * acc_ref[hh, :, qs] + jnp.dot(
                vaugT_ref[hh], pT, preferred_element_type=F32)
            m_ref[hh, :, qs] = m_new

    half = tq // 2
    assert half % LANES == 0

    def block(enabled, j, kaug_ref, vaugT_ref):
        first_seeing = j * tk - i * tq
        crosses_diagonal = (j + 1) * tk - 1 > i * tq
        upper_half_only = first_seeing >= half
        run = lambda masked, q_from: (lambda: step(j, kaug_ref, vaugT_ref, masked, q_from))
        pl.when(enabled & crosses_diagonal & upper_half_only)(run(True, half))
        pl.when(enabled & crosses_diagonal & jnp.logical_not(upper_half_only))(run(True, 0))
        pl.when(enabled & jnp.logical_not(crosses_diagonal))(run(False, 0))

    block(live, tja_ref[n], kaug_a_ref, vaugT_a_ref)
    block(has_b, tjb_ref[n], kaug_b_ref, vaugT_b_ref)

    final_j = jnp.where(has_b, tjb_ref[n], tja_ref[n])

    @pl.when(jnp.logical_and(live, final_j == last))
    def _():
        a0 = acc_ref[0]
        a1 = acc_ref[1]
        o0 = a0[0:FOX_HEAD_DIM, :] / a0[FOX_HEAD_DIM:FOX_HEAD_DIM + 1, :]
        o1 = a1[FOX_HEAD_DIM:, :] / a1[0:1, :]
        o_ref[...] = jnp.concatenate([o0, o1], axis=0).T.astype(BF16)


def _fox_tables(cum, norms, *, tq, tk):
    L = cum.shape[0]
    nq, nk = L // tq, L // tk
    pairs = FOX_HEADS // 2
    assert nq < 256 and nk < 256
    pp, ii, jj = np.meshgrid(np.arange(pairs), np.arange(nq), np.arange(nk), indexing="ij")
    causal = jj * tk <= (ii + 1) * tq - 1
    diagonal = (jj + 1) * tk - 1 > ii * tq
    code = jnp.asarray((pp << 16 | ii << 8 | jj).reshape(-1), jnp.int32)
    n_steps = int(((causal.reshape(pairs * nq, nk).sum(axis=1) + 1) // 2).sum())

    qn = jnp.sqrt(norms[:, :, 0:2, 0]).reshape(nk, FOX_HEADS)
    kn = jnp.sqrt(norms[:, :, 2:4, 0]).reshape(nk, FOX_HEADS)
    qn_i = qn.reshape(nq, tq // tk, FOX_HEADS).max(axis=1)
    kd_i = kn.reshape(nq, tq // tk, FOX_HEADS).max(axis=1)
    cq = cum[::tq] * LOG2E
    ck = cum[::tk] * LOG2E
    c_last = cum[tk - 1::tk] * LOG2E
    bound = qn_i[:, None, :] * (kn[None, :, :] + kd_i[:, None, :]) + (cq[:, None, :] - c_last[None, :, :])
    alive = (bound >= -FOX_SKIP_LOG2).reshape(nq, nk, pairs, 2).any(axis=-1)
    live = (jnp.asarray(causal) & (jnp.asarray(diagonal) | jnp.transpose(alive, (2, 0, 1)))).reshape(-1)
    live2 = live.reshape(pairs * nq, nk).astype(jnp.int32)
    rank = jnp.cumsum(live2, axis=1) - 1
    steps_g = (jnp.sum(live2, axis=1) + 1) // 2
    start_g = jnp.cumsum(steps_g) - steps_g
    step_of = (start_g[:, None] + rank // 2).reshape(-1)
    slot_of = (rank % 2).reshape(-1)
    count = jnp.sum(steps_g)
    step_id = jnp.arange(n_steps, dtype=jnp.int32)
    here = live[None, :] & (step_of[None, :] == step_id[:, None])
    in_a = here & (slot_of[None, :] == 0)
    in_b = here & (slot_of[None, :] == 1)
    code_a = jnp.sum(jnp.where(in_a, code[None, :], 0), axis=1)
    code_b = jnp.sum(jnp.where(in_b, code[None, :], 0), axis=1)
    active = step_id < count
    has_b = jnp.any(in_b, axis=1) & active
    code_a = jnp.where(active, code_a, jnp.max(jnp.where(live & (slot_of == 0), code, -1)))
    code_b = jnp.where(has_b, code_b, code_a)
    tp, ti, tja, tjb = code_a >> 16, (code_a >> 8) & 255, code_a & 255, code_b & 255
    first = jnp.concatenate([jnp.ones((1,), jnp.int32),
                             ((tp[1:] != tp[:-1]) | (ti[1:] != ti[:-1])).astype(jnp.int32)])
    return (tp, ti, tja, tjb, has_b.astype(jnp.int32), first, count.reshape(1).astype(jnp.int32),
            cq.reshape(-1), ck.reshape(-1))


def _fox(main, cum, *, tq, tk):
    L = main.shape[0]
    kaug, qaugT, vaugT, norms = _foxprep(main, cum, tb=tk)
    tables = _fox_tables(cum, norms, tq=tq, tk=tk)
    grid_spec = pltpu.PrefetchScalarGridSpec(
        num_scalar_prefetch=len(tables),
        grid=(tables[0].shape[0],),
        in_specs=[
            pl.BlockSpec((2, tk, LANES), lambda n, tp, ti, tja, tjb, *_: (tp[n], tja[n], 0)),
            pl.BlockSpec((2, LANES, tk), lambda n, tp, ti, tja, tjb, *_: (tp[n], 0, tja[n])),
            pl.BlockSpec((2, tk, LANES), lambda n, tp, ti, tja, tjb, *_: (tp[n], tjb[n], 0)),
            pl.BlockSpec((2, LANES, tk), lambda n, tp, ti, tja, tjb, *_: (tp[n], 0, tjb[n])),
            pl.BlockSpec((2, LANES, tq), lambda n, tp, ti, *_: (tp[n], 0, ti[n])),
        ],
        out_specs=pl.BlockSpec((tq, LANES), lambda n, tp, ti, *_: (ti[n], tp[n])),
        scratch_shapes=[
            pltpu.VMEM((2, 1, tq), F32),
            pltpu.VMEM((2, LANES, tq), F32),
        ],
    )
    return pl.pallas_call(
        functools.partial(_fox_kernel, tq=tq, tk=tk),
        grid_spec=grid_spec,
        out_shape=jax.ShapeDtypeStruct((L, FOX_INNER), BF16),
        compiler_params=pltpu.CompilerParams(
            dimension_semantics=("arbitrary",), vmem_limit_bytes=VMEM_LIMIT),
        name="fox",
    )(*tables, kaug, vaugT, kaug, vaugT, qaugT)


def _mid_kernel(x_ref, ys_ref, yf_ref, wo_ref, ln2_ref, wq_ref, k1_ref, k2_ref,
                x1_ref, h2T_ref, s1T_ref, s2T_ref):
    mix = jnp.dot(ys_ref[...], wo_ref[0:SSD_INNER, :], preferred_element_type=F32)
    mix = mix + jnp.dot(yf_ref[...], wo_ref[SSD_INNER:, :], preferred_element_type=F32)
    x1 = x_ref[...] + mix
    x1_ref[...] = x1
    h2 = x1 * lax.rsqrt(jnp.mean(x1 * x1, axis=-1, keepdims=True) + EPS) * ln2_ref[...]
    h2T_ref[...] = h2.T.astype(BF16)
    qp = jnp.dot(h2.astype(BF16), wq_ref[...], preferred_element_type=F32)
    half = PEER_KEY_DIM // 2
    for h in range(PEER_HEADS):
        q1 = qp[:, h * PEER_KEY_DIM:h * PEER_KEY_DIM + half].astype(BF16)
        q2 = qp[:, h * PEER_KEY_DIM + half:(h + 1) * PEER_KEY_DIM].astype(BF16)
        s1T_ref[h] = lax.dot_general(k1_ref[h], q1, _NT, preferred_element_type=F32)
        s2T_ref[h] = lax.dot_general(k2_ref[h], q2, _NT, preferred_element_type=F32)


def _mid(x2d, y_ssd, y_fox, w_out, ln2_w, wq, k1, k2, *, tm):
    L = x2d.shape[0]
    half = PEER_KEY_DIM // 2
    resident = lambda shape: pl.BlockSpec(shape, lambda i: (0,) * len(shape), pipeline_mode=pl.Buffered(1))
    return pl.pallas_call(
        _mid_kernel,
        grid=(L // tm,),
        in_specs=[
            pl.BlockSpec((tm, D_MODEL), lambda i: (i, 0)),
            pl.BlockSpec((tm, SSD_INNER), lambda i: (i, 0)),
            pl.BlockSpec((tm, FOX_INNER), lambda i: (i, 0)),
            resident((D_MODEL, D_MODEL)),
            resident((1, D_MODEL)),
            resident((D_MODEL, PEER_HEADS * PEER_KEY_DIM)),
            resident((PEER_HEADS, PEER_N_KEYS, half)),
            resident((PEER_HEADS, PEER_N_KEYS, half)),
        ],
        out_specs=[
            pl.BlockSpec((tm, D_MODEL), lambda i: (i, 0)),
            pl.BlockSpec((D_MODEL, tm), lambda i: (0, i)),
            pl.BlockSpec((PEER_HEADS, PEER_N_KEYS, tm), lambda i: (0, 0, i)),
            pl.BlockSpec((PEER_HEADS, PEER_N_KEYS, tm), lambda i: (0, 0, i)),
        ],
        out_shape=[
            jax.ShapeDtypeStruct((L, D_MODEL), F32),
            jax.ShapeDtypeStruct((D_MODEL, L), BF16),
            jax.ShapeDtypeStruct((PEER_HEADS, PEER_N_KEYS, L), F32),
            jax.ShapeDtypeStruct((PEER_HEADS, PEER_N_KEYS, L), F32),
        ],
        compiler_params=pltpu.CompilerParams(
            dimension_semantics=("parallel",), vmem_limit_bytes=VMEM_LIMIT),
        name="mid",
    )(x2d, y_ssd, y_fox, w_out, ln2_w.reshape(1, D_MODEL), wq, k1, k2)


def _top16(s, order, exact):
    axis = s.ndim - 2
    slot = lax.broadcasted_iota(jnp.int32, s.shape[:-2] + (PEER_TOPK, s.shape[-1]), axis)
    unranked = float(PEER_TOPK)

    if not exact:
        def next_value(r, carry):
            below, vals = carry
            m = jnp.max(jnp.where(s < below, s, -jnp.inf), axis=axis, keepdims=True)
            return m, jnp.where(slot == r, m, vals)

        ceiling = jnp.full(s.shape[:-2] + (1, s.shape[-1]), jnp.inf, F32)
        lowest, vals = lax.fori_loop(0, PEER_TOPK, next_value, (ceiling, jnp.zeros(slot.shape, F32)))
        taken = s >= lowest
        return None, vals, taken, jnp.sum(jnp.where(taken, 1.0, 0.0), axis=axis, keepdims=True)

    def extract(r, carry):
        s, rank, vals = carry
        m = jnp.max(s, axis=axis, keepdims=True)
        first = jnp.min(jnp.where(s == m, order, jnp.inf), axis=axis, keepdims=True)
        hit = order == first
        rank = jnp.where(hit, lax.convert_element_type(r, F32), rank)
        s = jnp.where(hit, -jnp.inf, s)
        return s, rank, jnp.where(slot == r, m, vals)

    init = (s, jnp.full(s.shape, unranked, F32), jnp.zeros(slot.shape, F32))
    _, rank, vals = lax.fori_loop(0, PEER_TOPK, extract, init)
    taken = rank < unranked
    return rank, vals, taken, jnp.sum(jnp.where(taken, 1.0, 0.0), axis=axis, keepdims=True)


def _topk_kernel(s1T_ref, s2T_ref, lim_ref, e1_ref, r2_ref, e2_ref):
    t = s1T_ref.shape[2]
    key_id = lax.broadcasted_iota(jnp.int32, (PEER_N_KEYS, t), 0).astype(F32)
    half_k = PEER_TOPK // 2
    pos_main = lax.broadcasted_iota(jnp.int32, (half_k * PEER_TOPK, t), 0)
    pos_tail = (lax.broadcasted_iota(jnp.int32, (half_k, t), 0) + half_k) * PEER_TOPK
    pos = jnp.concatenate([pos_main, pos_tail], axis=0).astype(F32)
    slot = lax.broadcasted_iota(jnp.int32, (PEER_TOPK, t), 0)

    def solve(h, exact):
        s1 = s1T_ref[h]
        s2 = s2T_ref[h]
        rank12, v12, _, n12 = _top16(jnp.stack([s1, s2]), key_id, exact)
        v1, v2 = v12[0], v12[1]

        cand = jnp.concatenate(
            [v1[a:a + 1, :] + v2 for a in range(half_k)] + [v1[half_k:, :] + v2[0:1, :]], axis=0)
        _, top_sums, taken_c, n_c = _top16(cand, pos, exact)
        sel = jnp.where(taken_c, 1.0, 0.0)
        z = jnp.sum(jnp.exp(top_sums - top_sums[0:1, :]), axis=0, keepdims=True)
        n_taken = jnp.maximum(jnp.maximum(n12[0], n12[1]), n_c)

        if exact:
            rank1, rank2 = rank12[0], rank12[1]
            is_rank1 = lambda a: rank1 == float(a)
        else:
            rank2 = jnp.full(s2.shape, float(PEER_TOPK), F32)
            for a in range(PEER_TOPK):
                rank2 = jnp.where(s2 == v2[a:a + 1, :], float(a), rank2)
            is_rank1 = lambda a: s1 == v1[a:a + 1, :]

        bcount = jnp.zeros((PEER_TOPK, t), F32)
        for a in range(half_k):
            cnt = jnp.sum(sel[a * PEER_TOPK:(a + 1) * PEER_TOPK, :], axis=0, keepdims=True)
            bcount = jnp.where(slot == a, cnt, bcount)
        tail = jnp.concatenate([jnp.zeros((half_k, t), F32), sel[half_k * PEER_TOPK:, :]], axis=0)
        bcount = jnp.where(slot >= half_k, tail, bcount)

        lim = jnp.zeros((PEER_N_KEYS, t), F32)
        for a in range(PEER_TOPK):
            lim = jnp.where(is_rank1(a), bcount[a:a + 1, :], lim)

        lim_ref[h] = lim
        e1_ref[h] = jnp.exp(s1 - v1[0:1, :])
        r2_ref[h] = rank2.astype(BF16)
        e2_ref[h] = (jnp.exp(s2 - v2[0:1, :]) / z).astype(BF16)
        return jnp.max(n_taken)

    def head(h, _):
        most = solve(h, exact=False)

        @pl.when(most > float(PEER_TOPK))
        def _():
            solve(h, exact=True)

        return 0

    lax.fori_loop(0, PEER_HEADS, head, 0)


def _topk(s1T, s2T, *, tt):
    L = s1T.shape[2]
    spec = pl.BlockSpec((PEER_HEADS, PEER_N_KEYS, tt), lambda i: (0, 0, i))
    shape = jax.ShapeDtypeStruct((PEER_HEADS, PEER_N_KEYS, L), F32)
    shape_b = jax.ShapeDtypeStruct((PEER_HEADS, PEER_N_KEYS, L), BF16)
    return pl.pallas_call(
        _topk_kernel,
        grid=(L // tt,),
        in_specs=[spec, spec],
        out_specs=[spec] * 4,
        out_shape=[shape, shape, shape_b, shape_b],
        compiler_params=pltpu.CompilerParams(
            dimension_semantics=("parallel",), vmem_limit_bytes=VMEM_LIMIT),
        name="topk",
    )(s1T, s2T)


def _peer_kernel(h2T_ref, u_ref, unext_ref, vT_ref, lim_ref, e1_ref, r2_ref, e2_ref, x1_ref, lnf_ref,
                 o_ref, acc_ref, act0_ref, *, et, sub):
    e = pl.program_id(1)
    n_sub = et // sub
    keys_per_sub = sub // PEER_N_KEYS

    def activation(rows_ref, s):
        return jnp.dot(rows_ref[s * sub:(s + 1) * sub, :], h2T_ref[...], preferred_element_type=F32)

    @pl.when(e == 0)
    def _():
        acc_ref[...] = jnp.zeros(acc_ref.shape, F32)
        act0_ref[...] = activation(u_ref, 0)

    def gate(s):
        parts = []
        for i in range(keys_per_sub):
            i1 = e * (et // PEER_N_KEYS) + s * keys_per_sub + i
            w = None
            for h in range(PEER_HEADS):
                lim_row = lim_ref[h, pl.ds(i1, 1), :].astype(BF16)
                e1_row = e1_ref[h, pl.ds(i1, 1), :].astype(BF16)
                term = jnp.where(r2_ref[h] < lim_row, e2_ref[h] * e1_row, jnp.zeros((), BF16))
                w = term if w is None else w + term
            parts.append(w)
        return jnp.concatenate(parts, axis=0)

    total = None
    a = act0_ref[...]
    for s in range(n_sub):
        a_next = activation(u_ref, s + 1) if s + 1 < n_sub else activation(unext_ref, 0)
        gelu = 0.5 * a * (1.0 + lax.erf(a * (2.0 ** -0.5)))
        wg = gelu.astype(BF16) * gate(s)
        part = jnp.dot(vT_ref[:, s * sub:(s + 1) * sub], wg, preferred_element_type=F32)
        total = part if total is None else total + part
        a = a_next
    act0_ref[...] = a
    acc_ref[...] += total

    @pl.when(e == pl.num_programs(1) - 1)
    def _():
        x2 = x1_ref[...] + acc_ref[...].T
        o_ref[...] = x2 * lax.rsqrt(jnp.mean(x2 * x2, axis=-1, keepdims=True) + EPS) * lnf_ref[...]


def _peer(h2T, u_b, vT_b, lim, e1, r2, e2, x1, lnf_w, *, tt, et, sub):
    L = x1.shape[0]
    tok = pl.BlockSpec((PEER_HEADS, PEER_N_KEYS, tt), lambda t, e: (0, 0, t), pipeline_mode=pl.Buffered(1))
    n_blocks = PEER_N_EXPERTS // et
    return pl.pallas_call(
        functools.partial(_peer_kernel, et=et, sub=sub),
        grid=(L // tt, n_blocks),
        in_specs=[
            pl.BlockSpec((D_MODEL, tt), lambda t, e: (0, t), pipeline_mode=pl.Buffered(1)),
            pl.BlockSpec((et, D_MODEL), lambda t, e: (e, 0)),
            pl.BlockSpec((sub, D_MODEL), lambda t, e: (jnp.minimum(e + 1, n_blocks - 1) * (et // sub), 0)),
            pl.BlockSpec((D_MODEL, et), lambda t, e: (0, e)),
            tok, tok, tok, tok,
            pl.BlockSpec((tt, D_MODEL), lambda t, e: (t, 0), pipeline_mode=pl.Buffered(1)),
            pl.BlockSpec((1, D_MODEL), lambda t, e: (0, 0)),
        ],
        out_specs=pl.BlockSpec((tt, D_MODEL), lambda t, e: (t, 0)),
        out_shape=jax.ShapeDtypeStruct((L, D_MODEL), F32),
        scratch_shapes=[
            pltpu.VMEM((D_MODEL, tt), F32),
            pltpu.VMEM((sub, tt), F32),
        ],
        compiler_params=pltpu.CompilerParams(
            dimension_semantics=("parallel", "arbitrary"), vmem_limit_bytes=VMEM_LIMIT),
        name="peer",
    )(h2T, u_b, u_b, vT_b, lim, e1, r2, e2, x1, lnf_w.reshape(1, D_MODEL))


def _tiles(L):
    return dict(
        inproj_tm=min(512, L), inproj_tn=2816,
        ssd_q=min(128, L),
        fox_tq=min(1024, L), fox_tk=min(512, L),
        mid_tm=min(256, L),
        topk_tt=min(512, L),
        peer_tt=min(512, L), peer_et=1024, peer_sub=512,
    )


def kernel(x, ln1_w, w_in, conv_w, conv_b, dt_bias, a_log, d_skip, ssd_norm_w, fox_f_bias,
           w_out, ln2_w, peer_wq, peer_k1, peer_k2, peer_u, peer_v, lnf_w):
    B, L, D = x.shape
    assert B == 1 and D == D_MODEL
    t = _tiles(L)
    x2d = x.reshape(L, D)

    c_z, c_xbc, c_dt, c_q = 0, SSD_INNER, 2560, 2576
    c_f = c_q + 3 * FOX_INNER
    w_main = jnp.concatenate([w_in[:, c_q:c_f], w_in[:, c_z:c_dt]], axis=1).astype(BF16)
    w_small = jnp.concatenate(
        [w_in[:, c_dt:c_q], w_in[:, c_f:], jnp.zeros((D, SMALL_DIM - SSD_HEADS - FOX_HEADS), F32)],
        axis=1).astype(BF16)

    main, small, smallT = _inproj(x2d, ln1_w, w_main, w_small, tm=t["inproj_tm"], tn=t["inproj_tn"])
    y_ssd, cum = _ssd(main, small, smallT, conv_w, conv_b, dt_bias, a_log, d_skip, ssd_norm_w,
                      fox_f_bias, q=t["ssd_q"])
    y_fox = _fox(main, cum, tq=t["fox_tq"], tk=t["fox_tk"])
    x1, h2T, s1T, s2T = _mid(x2d, y_ssd, y_fox, w_out.astype(BF16), ln2_w, peer_wq.astype(BF16),
                             peer_k1.astype(BF16), peer_k2.astype(BF16), tm=t["mid_tm"])
    lim, e1, r2, e2 = _topk(s1T, s2T, tt=t["topk_tt"])
    out = _peer(h2T, peer_u.astype(BF16), peer_v.T.astype(BF16), lim, e1, r2, e2, x1, lnf_w,
                tt=t["peer_tt"], et=t["peer_et"], sub=t["peer_sub"])
    return out.reshape(B, L, D)
```

```python
import functools

import numpy as np
import jax
import jax.numpy as jnp
from jax import lax
from jax.experimental import pallas as pl
from jax.experimental.pallas import tpu as pltpu

F32 = jnp.float32
BF16 = jnp.bfloat16

D_MODEL = 2048
EPS = 1e-6

SSD_HEAD_DIM = 64
SSD_INNER = 1024
SSD_HEADS = 16
SSD_GROUPS = 2
SSD_STATE = 128
SSD_CONV = 4
SSD_TAIL = 16
SSD_BC = 2 * SSD_GROUPS * SSD_STATE
SSD_GROUP_WIDTH = SSD_INNER // SSD_GROUPS

FOX_HEAD_DIM = 64
FOX_INNER = 1024
FOX_HEADS = 16

PEER_HEADS = 8
PEER_N_KEYS = 128
PEER_N_EXPERTS = PEER_N_KEYS * PEER_N_KEYS
PEER_KEY_DIM = 256
PEER_TOPK = 16

COL_Q = 0
COL_K = 1024
COL_V = 2048
COL_Z = 3072
COL_XS = 4096
COL_BC = 5120
MAIN_DIM = 5632
SMALL_DIM = 128

LANES = 128
VMEM_LIMIT = 56 * 1024 * 1024

NEG_BIG = -1e30
LOG2E = 1.4426950408889634

_NT = (((1,), (1,)), ((), ()))
_TN = (((0,), (0,)), ((), ()))


def _softplus(x):
    return jnp.maximum(x, 0.0) + jnp.log1p(jnp.exp(-jnp.abs(x)))


def _silu(x):
    return x * (0.5 * jnp.tanh(0.5 * x) + 0.5)


def _split3(x):
    hi = x.astype(BF16)
    r = x - hi.astype(F32)
    mid = r.astype(BF16)
    lo = (r - mid.astype(F32)).astype(BF16)
    return hi, mid, lo


def _dot3_lhs(x, w):
    a, b, c = _split3(x)
    d = lambda p: jnp.dot(p, w, preferred_element_type=F32)
    return d(a) + d(b) + d(c)


def _dot3_rhs(w, x):
    a, b, c = _split3(x)
    d = lambda p: jnp.dot(w, p, preferred_element_type=F32)
    return d(a) + d(b) + d(c)


def _inproj_kernel(x_ref, lnw_ref, wm_ref, ws_ref, wsT_ref, main_ref, small_ref, smallT_ref, h_ref):
    @pl.when(pl.program_id(1) == 0)
    def _():
        x = x_ref[...]
        h = x * lax.rsqrt(jnp.mean(x * x, axis=-1, keepdims=True) + EPS) * lnw_ref[...]
        hb = h.astype(BF16)
        h_ref[...] = hb
        small_ref[...] = jnp.dot(hb, ws_ref[...], preferred_element_type=F32)
        smallT_ref[...] = lax.dot_general(wsT_ref[...], hb, _NT, preferred_element_type=F32)

    main_ref[...] = jnp.dot(h_ref[...], wm_ref[...], preferred_element_type=F32).astype(BF16)


def _inproj(x2d, ln1_w, w_main, w_small, *, tm, tn):
    L = x2d.shape[0]
    return pl.pallas_call(
        _inproj_kernel,
        grid=(L // tm, MAIN_DIM // tn),
        in_specs=[
            pl.BlockSpec((tm, D_MODEL), lambda i, j: (i, 0)),
            pl.BlockSpec((1, D_MODEL), lambda i, j: (0, 0)),
            pl.BlockSpec((D_MODEL, tn), lambda i, j: (0, j)),
            pl.BlockSpec((D_MODEL, SMALL_DIM), lambda i, j: (0, 0)),
            pl.BlockSpec((SMALL_DIM, D_MODEL), lambda i, j: (0, 0)),
        ],
        out_specs=[
            pl.BlockSpec((tm, tn), lambda i, j: (i, j)),
            pl.BlockSpec((tm, SMALL_DIM), lambda i, j: (i, 0)),
            pl.BlockSpec((SMALL_DIM, tm), lambda i, j: (0, i)),
        ],
        out_shape=[
            jax.ShapeDtypeStruct((L, MAIN_DIM), BF16),
            jax.ShapeDtypeStruct((L, SMALL_DIM), F32),
            jax.ShapeDtypeStruct((SMALL_DIM, L), F32),
        ],
        scratch_shapes=[pltpu.VMEM((tm, D_MODEL), BF16)],
        compiler_params=pltpu.CompilerParams(
            dimension_semantics=("parallel", "arbitrary"), vmem_limit_bytes=VMEM_LIMIT),
        name="inproj",
    )(x2d, ln1_w.reshape(1, D_MODEL), w_main, w_small, w_small.T)


def _ssd_kernel(z_ref, xs_ref, bc_ref, small_ref, smallT_ref, *rest, q, chunks):
    consts, (y_ref, cum_ref), scratch = rest[:12], rest[12:14], rest[14:]
    extx_ref, extbc_ref, state_ref, fcarry_ref = scratch

    @pl.when(pl.program_id(0) == 0)
    def _():
        extx_ref[0:SSD_TAIL, :] = jnp.zeros((SSD_TAIL, SSD_INNER), F32)
        extbc_ref[0:SSD_TAIL, :] = jnp.zeros((SSD_TAIL, SSD_BC), F32)
        state_ref[...] = jnp.zeros(state_ref.shape, F32)
        fcarry_ref[...] = jnp.zeros(fcarry_ref.shape, F32)

    for r in range(chunks):
        rows = pl.ds(r * q, q)
        _ssd_chunk(z_ref.at[rows], xs_ref.at[rows], bc_ref.at[rows], small_ref.at[rows],
                   smallT_ref.at[:, rows], *consts, y_ref.at[rows], cum_ref.at[rows], *scratch, q=q)


def _ssd_chunk(z_ref, xs_ref, bc_ref, small_ref, smallT_ref,
               cwx_ref, cbx_ref, cwbc_ref, cbbc_ref,
               dtb_ref, dtbT_ref, alog_ref, alogT_ref, fb_ref,
               dfull_ref, normw_ref, expand_ref,
               y_ref, cum_ref,
               extx_ref, extbc_ref, state_ref, fcarry_ref, *, q):
    tail = SSD_TAIL

    extx_ref[tail:tail + q, :] = xs_ref[...].astype(F32)
    extbc_ref[tail:tail + q, :] = bc_ref[...].astype(F32)

    srow = lax.broadcasted_iota(jnp.int32, (q, q + tail), 0)
    scol = lax.broadcasted_iota(jnp.int32, (q, q + tail), 1)

    def conv(ext_ref, w_ref, b_ref):
        acc = b_ref[...] + ext_ref[tail:tail + q, :] * w_ref[SSD_CONV - 1:SSD_CONV, :]
        ext = ext_ref[...].astype(BF16)
        for k in range(SSD_CONV - 1):
            off = tail - (SSD_CONV - 1) + k
            shift = jnp.where(scol == srow + off, 1.0, 0.0).astype(BF16)
            acc = acc + jnp.dot(shift, ext, preferred_element_type=F32) * w_ref[k:k + 1, :]
        return acc

    xs = _silu(conv(extx_ref, cwx_ref, cbx_ref))
    bc = _silu(conv(extbc_ref, cwbc_ref, cbbc_ref))
    extx_ref[0:tail, :] = extx_ref[q:q + tail, :]
    extbc_ref[0:tail, :] = extbc_ref[q:q + tail, :]

    row = lax.broadcasted_iota(jnp.int32, (q, q), 0)
    col = lax.broadcasted_iota(jnp.int32, (q, q), 1)
    lower = row >= col
    tri = lower.astype(BF16)
    upper = (row <= col).astype(BF16)

    small = small_ref[...]
    smallT = smallT_ref[...]
    neg_a = -jnp.exp(alog_ref[...])
    neg_aT = -jnp.exp(alogT_ref[...])
    dt = _softplus(small[:, 0:SSD_HEADS] + dtb_ref[...])
    dtT = _softplus(smallT[0:SSD_HEADS, :] + dtbT_ref[...])
    a_cum = _dot3_rhs(tri, dt * neg_a)
    a_cumT = _dot3_lhs(dtT * neg_aT, upper)
    a_last = a_cum[q - 1:q, :]

    logf = -_softplus(-(small[:, SSD_HEADS:SSD_HEADS + FOX_HEADS] + fb_ref[...]))
    cumf = fcarry_ref[...] + _dot3_rhs(tri, logf)
    cum_ref[...] = cumf
    fcarry_ref[...] = cumf[q - 1:q, :]

    expand = expand_ref[...]
    dt_full = _dot3_lhs(dt, expand)
    ea_full = _dot3_lhs(jnp.exp(a_cum), expand)
    ds_full = _dot3_lhs(jnp.exp(a_last - a_cum), expand)

    xdt = xs * dt_full
    xdt_b = xdt.astype(BF16)
    xw_b = (xdt * ds_full).astype(BF16)
    bc_b = bc.astype(BF16)
    lane = lax.broadcasted_iota(jnp.int32, (q, LANES), 1)

    pieces = []
    for g in range(SSD_GROUPS):
        b_g = bc_b[:, g * SSD_STATE:(g + 1) * SSD_STATE]
        c_g = bc_b[:, (SSD_GROUPS + g) * SSD_STATE:(SSD_GROUPS + g + 1) * SSD_STATE]
        cb = lax.dot_general(c_g, b_g, _NT, preferred_element_type=F32)
        heads_per_group = SSD_HEADS // SSD_GROUPS
        for pp in range(heads_per_group // 2):
            pair = g * (heads_per_group // 2) + pp
            xp = xdt_b[:, pair * LANES:(pair + 1) * LANES]
            ys = []
            for hh in range(2):
                h = 2 * pair + hh
                seg = a_cum[:, h:h + 1] - a_cumT[h:h + 1, :]
                decay = jnp.exp(jnp.where(lower, seg, -jnp.inf))
                m = (cb * decay).astype(BF16)
                ys.append(jnp.dot(m, xp, preferred_element_type=F32))
            pieces.append(jnp.where(lane < SSD_HEAD_DIM, ys[0], ys[1]))
    y_diag = jnp.concatenate(pieces, axis=1)

    offs = []
    for g in range(SSD_GROUPS):
        b_g = bc_b[:, g * SSD_STATE:(g + 1) * SSD_STATE]
        c_g = bc_b[:, (SSD_GROUPS + g) * SSD_STATE:(SSD_GROUPS + g + 1) * SSD_STATE]
        gs = slice(g * SSD_GROUP_WIDTH, (g + 1) * SSD_GROUP_WIDTH)
        st = state_ref[g]
        offs.append(jnp.dot(c_g, st.astype(BF16), preferred_element_type=F32))
        state_ref[g] = st * ea_full[q - 1:q, gs] + lax.dot_general(
            b_g, xw_b[:, gs], _TN, preferred_element_type=F32)
    y_off = jnp.concatenate(offs, axis=1) * ea_full

    y = y_diag + y_off + dfull_ref[...] * xs
    y = y * _silu(z_ref[...].astype(F32))
    normed = []
    for g in range(SSD_GROUPS):
        yg = y[:, g * SSD_GROUP_WIDTH:(g + 1) * SSD_GROUP_WIDTH]
        normed.append(yg * lax.rsqrt(jnp.mean(yg * yg, axis=-1, keepdims=True) + EPS))
    y_ref[...] = (jnp.concatenate(normed, axis=1) * normw_ref[...]).astype(BF16)


def _ssd(main, small, smallT, conv_w, conv_b, dt_bias, a_log, d_skip, ssd_norm_w, fox_f_bias, *, q, chunks):
    L = main.shape[0]
    rows = q * chunks
    expand = (jnp.arange(SSD_INNER)[None, :] // SSD_HEAD_DIM == jnp.arange(SSD_HEADS)[:, None]).astype(BF16)
    const = lambda shape: pl.BlockSpec(shape, lambda c: (0,) * len(shape))
    return pl.pallas_call(
        functools.partial(_ssd_kernel, q=q, chunks=chunks),
        grid=(L // rows,),
        in_specs=[
            pl.BlockSpec((rows, SSD_INNER), lambda c: (c, COL_Z // SSD_INNER)),
            pl.BlockSpec((rows, SSD_INNER), lambda c: (c, COL_XS // SSD_INNER)),
            pl.BlockSpec((rows, SSD_BC), lambda c: (c, COL_BC // SSD_BC)),
            pl.BlockSpec((rows, SMALL_DIM), lambda c: (c, 0)),
            pl.BlockSpec((SMALL_DIM, rows), lambda c: (0, c)),
            const((SSD_CONV, SSD_INNER)), const((1, SSD_INNER)),
            const((SSD_CONV, SSD_BC)), const((1, SSD_BC)),
            const((1, SSD_HEADS)), const((SSD_HEADS, 1)),
            const((1, SSD_HEADS)), const((SSD_HEADS, 1)),
            const((1, FOX_HEADS)),
            const((1, SSD_INNER)), const((1, SSD_INNER)),
            const((SSD_HEADS, SSD_INNER)),
        ],
        out_specs=[
            pl.BlockSpec((rows, SSD_INNER), lambda c: (c, 0)),
            pl.BlockSpec((rows, FOX_HEADS), lambda c: (c, 0)),
        ],
        out_shape=[
            jax.ShapeDtypeStruct((L, SSD_INNER), BF16),
            jax.ShapeDtypeStruct((L, FOX_HEADS), F32),
        ],
        scratch_shapes=[
            pltpu.VMEM((q + SSD_TAIL, SSD_INNER), F32),
            pltpu.VMEM((q + SSD_TAIL, SSD_BC), F32),
            pltpu.VMEM((SSD_GROUPS, SSD_STATE, SSD_GROUP_WIDTH), F32),
            pltpu.VMEM((1, FOX_HEADS), F32),
        ],
        compiler_params=pltpu.CompilerParams(
            dimension_semantics=("arbitrary",), vmem_limit_bytes=VMEM_LIMIT),
        name="ssd",
    )(main, main, main, small, smallT,
      conv_w[:, :SSD_INNER], conv_b[:SSD_INNER].reshape(1, -1),
      conv_w[:, SSD_INNER:], conv_b[SSD_INNER:].reshape(1, -1),
      dt_bias.reshape(1, -1), dt_bias.reshape(-1, 1),
      a_log.reshape(1, -1), a_log.reshape(-1, 1),
      fox_f_bias.reshape(1, -1),
      jnp.repeat(d_skip, SSD_HEAD_DIM).reshape(1, -1), ssd_norm_w.reshape(1, -1),
      expand)


FOX_AUG = 3
FOX_SKIP_LOG2 = 160.0


def _foxprep_kernel(q_ref, k_ref, v_ref, cum_ref, kaug_ref, qaugT_ref, vaugT_ref, norm_ref):
    stat_row = lax.broadcasted_iota(jnp.int32, (8, LANES), 0)
    tb = q_ref.shape[0]
    lane = lax.broadcasted_iota(jnp.int32, (tb, LANES), 1)
    cum = cum_ref[...]
    rel = (cum[0:1, :] - cum) * LOG2E
    hrow = lax.broadcasted_iota(jnp.int32, (FOX_HEADS, FOX_INNER), 0)
    hcol = lax.broadcasted_iota(jnp.int32, (FOX_HEADS, FOX_INNER), 1)
    spare0 = (hrow // 2) * LANES + jnp.where(hrow % 2 == 0, FOX_HEAD_DIM, 0)
    aug_all = None
    for r, piece in enumerate(_split3(rel)):
        place = jnp.where(hcol == spare0 + r, 1.0, 0.0).astype(BF16)
        term = jnp.dot(piece, place, preferred_element_type=F32)
        aug_all = term if aug_all is None else aug_all + term
    aug_all = aug_all.astype(BF16)
    eye = jnp.where(lax.broadcasted_iota(jnp.int32, (LANES, LANES), 0)
                    == lax.broadcasted_iota(jnp.int32, (LANES, LANES), 1), 1.0, 0.0).astype(BF16)
    transpose = lambda xb: lax.dot_general(eye, xb, _NT, preferred_element_type=F32)
    trow = lax.broadcasted_iota(jnp.int32, (LANES, tb), 0)
    for p in range(FOX_HEADS // 2):
        cols = slice(p * LANES, (p + 1) * LANES)
        q = q_ref[:, cols].astype(F32) * (FOX_HEAD_DIM ** -0.5 * LOG2E)
        k = k_ref[:, cols]
        v = v_ref[:, cols].astype(F32)
        kT = transpose(k)
        stats = jnp.zeros((8, LANES), F32)
        for hh in range(2):
            head_lanes = (lane < FOX_HEAD_DIM) if hh == 0 else (lane >= FOX_HEAD_DIM)
            base = FOX_HEAD_DIM if hh == 0 else 0
            h = 2 * p + hh
            kaug_ref[h] = jnp.where(head_lanes, k, aug_all[:, cols])
            is_aug = (lane >= base) & (lane < base + FOX_AUG)
            qa = jnp.where(head_lanes, q, jnp.where(is_aug, 1.0, 0.0)).astype(BF16)
            va = jnp.where(head_lanes, v, jnp.where(lane == base, 1.0, 0.0)).astype(BF16)
            qaT = transpose(qa)
            qaugT_ref[h] = qaT.astype(BF16)
            vaugT_ref[h] = transpose(va).astype(BF16)
            head_rows = (trow < FOX_HEAD_DIM) if hh == 0 else (trow >= FOX_HEAD_DIM)
            sq_norms = lambda xT: jnp.sum(jnp.where(head_rows, xT * xT, 0.0), axis=0, keepdims=True)
            qn2 = jnp.max(sq_norms(qaT), axis=1, keepdims=True)
            kn2 = jnp.max(sq_norms(kT), axis=1, keepdims=True)
            stats = jnp.where(stat_row == hh, qn2, jnp.where(stat_row == 2 + hh, kn2, stats))
        norm_ref[p] = stats


def _foxprep(main, cum, *, tb):
    L = main.shape[0]
    pairs = FOX_HEADS // 2
    return pl.pallas_call(
        _foxprep_kernel,
        grid=(L // tb,),
        in_specs=[
            pl.BlockSpec((tb, FOX_INNER), lambda b: (b, COL_Q // FOX_INNER)),
            pl.BlockSpec((tb, FOX_INNER), lambda b: (b, COL_K // FOX_INNER)),
            pl.BlockSpec((tb, FOX_INNER), lambda b: (b, COL_V // FOX_INNER)),
            pl.BlockSpec((tb, FOX_HEADS), lambda b: (b, 0)),
        ],
        out_specs=[
            pl.BlockSpec((FOX_HEADS, tb, LANES), lambda b: (0, b, 0)),
            pl.BlockSpec((FOX_HEADS, LANES, tb), lambda b: (0, 0, b)),
            pl.BlockSpec((FOX_HEADS, LANES, tb), lambda b: (0, 0, b)),
            pl.BlockSpec((None, pairs, 8, LANES), lambda b: (b, 0, 0, 0)),
        ],
        out_shape=[
            jax.ShapeDtypeStruct((FOX_HEADS, L, LANES), BF16),
            jax.ShapeDtypeStruct((FOX_HEADS, LANES, L), BF16),
            jax.ShapeDtypeStruct((FOX_HEADS, LANES, L), BF16),
            jax.ShapeDtypeStruct((L // tb, pairs, 8, LANES), F32),
        ],
        compiler_params=pltpu.CompilerParams(
            dimension_semantics=("parallel",), vmem_limit_bytes=VMEM_LIMIT),
        name="foxprep",
    )(main, main, main, cum)


def _fox_kernel(tp_ref, ti_ref, tja_ref, tjb_ref, hasb_ref, first_ref, count_ref, cq_ref, ck_ref,
                kaug_a_ref, vaugT_a_ref, kaug_b_ref, vaugT_b_ref, qaugT_ref, o_ref, m_ref, acc_ref, *, tq, tk):
    n = pl.program_id(0)
    p = tp_ref[n]
    i = ti_ref[n]
    live = n < count_ref[0]
    has_b = jnp.logical_and(live, hasb_ref[n] == 1)
    last = ((i + 1) * tq - 1) // tk

    @pl.when(jnp.logical_and(live, first_ref[n] == 1))
    def _():
        m_ref[...] = jnp.full(m_ref.shape, NEG_BIG, F32)
        acc_ref[...] = jnp.zeros(acc_ref.shape, F32)

    def step(j, kaug_ref, vaugT_ref, masked, q_from):
        qs = slice(q_from, tq)
        if masked:
            key = j * tk + lax.broadcasted_iota(jnp.int32, (tk, tq - q_from), 0)
            qry = i * tq + q_from + lax.broadcasted_iota(jnp.int32, (tk, tq - q_from), 1)
            valid = key <= qry
        scores = [jnp.dot(kaug_ref[hh], qaugT_ref[hh, :, qs], preferred_element_type=F32) for hh in range(2)]
        for hh in range(2):
            h = 2 * p + hh
            d = cq_ref[i * FOX_HEADS + h] - ck_ref[j * FOX_HEADS + h]
            sT = scores[hh]
            if masked:
                sT = jnp.where(valid, sT, NEG_BIG)
            m_prev = m_ref[hh, :, qs]
            m_new = jnp.maximum(m_prev, jnp.max(sT, axis=0, keepdims=True) + d)
            alpha = jnp.exp2(m_prev - m_new)
            pT = jnp.exp2((sT - (m_new - d)).astype(BF16))
            acc_ref[hh, :, qs] = alpha * acc_ref[hh, :, qs] + jnp.dot(
                vaugT_ref[hh], pT, preferred_element_type=F32)
            m_ref[hh, :, qs] = m_new

    half = tq // 2
    assert half % LANES == 0

    def block(enabled, j, kaug_ref, vaugT_ref):
        first_seeing = j * tk - i * tq
        crosses_diagonal = (j + 1) * tk - 1 > i * tq
        upper_half_only = first_seeing >= half
        run = lambda masked, q_from: (lambda: step(j, kaug_ref, vaugT_ref, masked, q_from))
        pl.when(enabled & crosses_diagonal & upper_half_only)(run(True, half))
        pl.when(enabled & crosses_diagonal & jnp.logical_not(upper_half_only))(run(True, 0))
        pl.when(enabled & jnp.logical_not(crosses_diagonal))(run(False, 0))

    block(live, tja_ref[n], kaug_a_ref, vaugT_a_ref)
    block(has_b, tjb_ref[n], kaug_b_ref, vaugT_b_ref)

    final_j = jnp.where(has_b, tjb_ref[n], tja_ref[n])

    @pl.when(jnp.logical_and(live, final_j == last))
    def _():
        a0 = acc_ref[0]
        a1 = acc_ref[1]
        o0 = a0[0:FOX_HEAD_DIM, :] / a0[FOX_HEAD_DIM:FOX_HEAD_DIM + 1, :]
        o1 = a1[FOX_HEAD_DIM:, :] / a1[0:1, :]
        o_ref[...] = jnp.concatenate([o0, o1], axis=0).T.astype(BF16)


def _fox_tables(cum, norms, *, tq, tk):
    L = cum.shape[0]
    nq, nk = L // tq, L // tk
    pairs = FOX_HEADS // 2
    assert nq < 256 and nk < 256
    pp, ii, jj = np.meshgrid(np.arange(pairs), np.arange(nq), np.arange(nk), indexing="ij")
    causal = jj * tk <= (ii + 1) * tq - 1
    diagonal = (jj + 1) * tk - 1 > ii * tq
    code = jnp.asarray((pp << 16 | ii << 8 | jj).reshape(-1), jnp.int32)
    n_steps = int(((causal.reshape(pairs * nq, nk).sum(axis=1) + 1) // 2).sum())

    qn = jnp.sqrt(norms[:, :, 0:2, 0]).reshape(nk, FOX_HEADS)
    kn = jnp.sqrt(norms[:, :, 2:4, 0]).reshape(nk, FOX_HEADS)
    qn_i = qn.reshape(nq, tq // tk, FOX_HEADS).max(axis=1)
    kd_i = kn.reshape(nq, tq // tk, FOX_HEADS).max(axis=1)
    cq = cum[::tq] * LOG2E
    ck = cum[::tk] * LOG2E
    c_last = cum[tk - 1::tk] * LOG2E
    bound = qn_i[:, None, :] * (kn[None, :, :] + kd_i[:, None, :]) + (cq[:, None, :] - c_last[None, :, :])
    alive = (bound >= -FOX_SKIP_LOG2).reshape(nq, nk, pairs, 2).any(axis=-1)
    live = (jnp.asarray(causal) & (jnp.asarray(diagonal) | jnp.transpose(alive, (2, 0, 1)))).reshape(-1)
    live2 = live.reshape(pairs * nq, nk).astype(jnp.int32)
    rank = jnp.cumsum(live2, axis=1) - 1
    steps_g = (jnp.sum(live2, axis=1) + 1) // 2
    start_g = jnp.cumsum(steps_g) - steps_g
    step_of = (start_g[:, None] + rank // 2).reshape(-1)
    slot_of = (rank % 2).reshape(-1)
    count = jnp.sum(steps_g)
    step_id = jnp.arange(n_steps, dtype=jnp.int32)
    here = live[None, :] & (step_of[None, :] == step_id[:, None])
    in_a = here & (slot_of[None, :] == 0)
    in_b = here & (slot_of[None, :] == 1)
    code_a = jnp.sum(jnp.where(in_a, code[None, :], 0), axis=1)
    code_b = jnp.sum(jnp.where(in_b, code[None, :], 0), axis=1)
    active = step_id < count
    has_b = jnp.any(in_b, axis=1) & active
    code_a = jnp.where(active, code_a, jnp.max(jnp.where(live & (slot_of == 0), code, -1)))
    code_b = jnp.where(has_b, code_b, code_a)
    tp, ti, tja, tjb = code_a >> 16, (code_a >> 8) & 255, code_a & 255, code_b & 255
    first = jnp.concatenate([jnp.ones((1,), jnp.int32),
                             ((tp[1:] != tp[:-1]) | (ti[1:] != ti[:-1])).astype(jnp.int32)])
    return (tp, ti, tja, tjb, has_b.astype(jnp.int32), first, count.reshape(1).astype(jnp.int32),
            cq.reshape(-1), ck.reshape(-1))


def _fox(main, cum, *, tq, tk):
    L = main.shape[0]
    kaug, qaugT, vaugT, norms = _foxprep(main, cum, tb=tk)
    tables = _fox_tables(cum, norms, tq=tq, tk=tk)
    grid_spec = pltpu.PrefetchScalarGridSpec(
        num_scalar_prefetch=len(tables),
        grid=(tables[0].shape[0],),
        in_specs=[
            pl.BlockSpec((2, tk, LANES), lambda n, tp, ti, tja, tjb, *_: (tp[n], tja[n], 0)),
            pl.BlockSpec((2, LANES, tk), lambda n, tp, ti, tja, tjb, *_: (tp[n], 0, tja[n])),
            pl.BlockSpec((2, tk, LANES), lambda n, tp, ti, tja, tjb, *_: (tp[n], tjb[n], 0)),
            pl.BlockSpec((2, LANES, tk), lambda n, tp, ti, tja, tjb, *_: (tp[n], 0, tjb[n])),
            pl.BlockSpec((2, LANES, tq), lambda n, tp, ti, *_: (tp[n], 0, ti[n])),
        ],
        out_specs=pl.BlockSpec((tq, LANES), lambda n, tp, ti, *_: (ti[n], tp[n])),
        scratch_shapes=[
            pltpu.VMEM((2, 1, tq), F32),
            pltpu.VMEM((2, LANES, tq), F32),
        ],
    )
    return pl.pallas_call(
        functools.partial(_fox_kernel, tq=tq, tk=tk),
        grid_spec=grid_spec,
        out_shape=jax.ShapeDtypeStruct((L, FOX_INNER), BF16),
        compiler_params=pltpu.CompilerParams(
            dimension_semantics=("arbitrary",), vmem_limit_bytes=VMEM_LIMIT),
        name="fox",
    )(*tables, kaug, vaugT, kaug, vaugT, qaugT)


def _mid_kernel(x_ref, ys_ref, yf_ref, wo_ref, ln2_ref, wq_ref, k1_ref, k2_ref,
                x1_ref, h2T_ref, s1T_ref, s2T_ref):
    mix = jnp.dot(ys_ref[...], wo_ref[0:SSD_INNER, :], preferred_element_type=F32)
    mix = mix + jnp.dot(yf_ref[...], wo_ref[SSD_INNER:, :], preferred_element_type=F32)
    x1 = x_ref[...] + mix
    x1_ref[...] = x1
    h2 = x1 * lax.rsqrt(jnp.mean(x1 * x1, axis=-1, keepdims=True) + EPS) * ln2_ref[...]
    h2T_ref[...] = h2.T.astype(BF16)
    qp = jnp.dot(h2.astype(BF16), wq_ref[...], preferred_element_type=F32)
    half = PEER_KEY_DIM // 2
    for h in range(PEER_HEADS):
        q1 = qp[:, h * PEER_KEY_DIM:h * PEER_KEY_DIM + half].astype(BF16)
        q2 = qp[:, h * PEER_KEY_DIM + half:(h + 1) * PEER_KEY_DIM].astype(BF16)
        s1T_ref[h] = lax.dot_general(k1_ref[h], q1, _NT, preferred_element_type=F32)
        s2T_ref[h] = lax.dot_general(k2_ref[h], q2, _NT, preferred_element_type=F32)


def _mid(x2d, y_ssd, y_fox, w_out, ln2_w, wq, k1, k2, *, tm):
    L = x2d.shape[0]
    half = PEER_KEY_DIM // 2
    resident = lambda shape: pl.BlockSpec(shape, lambda i: (0,) * len(shape), pipeline_mode=pl.Buffered(1))
    return pl.pallas_call(
        _mid_kernel,
        grid=(L // tm,),
        in_specs=[
            pl.BlockSpec((tm, D_MODEL), lambda i: (i, 0)),
            pl.BlockSpec((tm, SSD_INNER), lambda i: (i, 0)),
            pl.BlockSpec((tm, FOX_INNER), lambda i: (i, 0)),
            resident((D_MODEL, D_MODEL)),
            resident((1, D_MODEL)),
            resident((D_MODEL, PEER_HEADS * PEER_KEY_DIM)),
            resident((PEER_HEADS, PEER_N_KEYS, half)),
            resident((PEER_HEADS, PEER_N_KEYS, half)),
        ],
        out_specs=[
            pl.BlockSpec((tm, D_MODEL), lambda i: (i, 0)),
            pl.BlockSpec((D_MODEL, tm), lambda i: (0, i)),
            pl.BlockSpec((PEER_HEADS, PEER_N_KEYS, tm), lambda i: (0, 0, i)),
            pl.BlockSpec((PEER_HEADS, PEER_N_KEYS, tm), lambda i: (0, 0, i)),
        ],
        out_shape=[
            jax.ShapeDtypeStruct((L, D_MODEL), F32),
            jax.ShapeDtypeStruct((D_MODEL, L), BF16),
            jax.ShapeDtypeStruct((PEER_HEADS, PEER_N_KEYS, L), F32),
            jax.ShapeDtypeStruct((PEER_HEADS, PEER_N_KEYS, L), F32),
        ],
        compiler_params=pltpu.CompilerParams(
            dimension_semantics=("parallel",), vmem_limit_bytes=VMEM_LIMIT),
        name="mid",
    )(x2d, y_ssd, y_fox, w_out, ln2_w.reshape(1, D_MODEL), wq, k1, k2)


def _top16(s, order, exact):
    axis = s.ndim - 2
    slot = lax.broadcasted_iota(jnp.int32, s.shape[:-2] + (PEER_TOPK, s.shape[-1]), axis)
    unranked = float(PEER_TOPK)

    if not exact:
        def next_value(r, carry):
            below, vals = carry
            m = jnp.max(jnp.where(s < below, s, -jnp.inf), axis=axis, keepdims=True)
            return m, jnp.where(slot == r, m, vals)

        ceiling = jnp.full(s.shape[:-2] + (1, s.shape[-1]), jnp.inf, F32)
        lowest, vals = lax.fori_loop(0, PEER_TOPK, next_value, (ceiling, jnp.zeros(slot.shape, F32)))
        taken = s >= lowest
        return None, vals, taken, jnp.sum(jnp.where(taken, 1.0, 0.0), axis=axis, keepdims=True)

    def extract(r, carry):
        s, rank, vals = carry
        m = jnp.max(s, axis=axis, keepdims=True)
        first = jnp.min(jnp.where(s == m, order, jnp.inf), axis=axis, keepdims=True)
        hit = order == first
        rank = jnp.where(hit, lax.convert_element_type(r, F32), rank)
        s = jnp.where(hit, -jnp.inf, s)
        return s, rank, jnp.where(slot == r, m, vals)

    init = (s, jnp.full(s.shape, unranked, F32), jnp.zeros(slot.shape, F32))
    _, rank, vals = lax.fori_loop(0, PEER_TOPK, extract, init)
    taken = rank < unranked
    return rank, vals, taken, jnp.sum(jnp.where(taken, 1.0, 0.0), axis=axis, keepdims=True)


def _topk_kernel(s1T_ref, s2T_ref, lim_ref, e1_ref, r2_ref, e2_ref):
    t = s1T_ref.shape[2]
    key_id = lax.broadcasted_iota(jnp.int32, (PEER_N_KEYS, t), 0).astype(F32)
    half_k = PEER_TOPK // 2
    pos_main = lax.broadcasted_iota(jnp.int32, (half_k * PEER_TOPK, t), 0)
    pos_tail = (lax.broadcasted_iota(jnp.int32, (half_k, t), 0) + half_k) * PEER_TOPK
    pos = jnp.concatenate([pos_main, pos_tail], axis=0).astype(F32)
    slot = lax.broadcasted_iota(jnp.int32, (PEER_TOPK, t), 0)

    def solve(h, exact):
        s1 = s1T_ref[h]
        s2 = s2T_ref[h]
        rank12, v12, _, n12 = _top16(jnp.stack([s1, s2]), key_id, exact)
        v1, v2 = v12[0], v12[1]

        cand = jnp.concatenate(
            [v1[a:a + 1, :] + v2 for a in range(half_k)] + [v1[half_k:, :] + v2[0:1, :]], axis=0)
        _, top_sums, taken_c, n_c = _top16(cand, pos, exact)
        sel = jnp.where(taken_c, 1.0, 0.0)
        z = jnp.sum(jnp.exp(top_sums - top_sums[0:1, :]), axis=0, keepdims=True)
        n_taken = jnp.maximum(jnp.maximum(n12[0], n12[1]), n_c)

        if exact:
            rank1, rank2 = rank12[0], rank12[1]
            is_rank1 = lambda a: rank1 == float(a)
        else:
            rank2 = jnp.full(s2.shape, float(PEER_TOPK), F32)
            for a in range(PEER_TOPK):
                rank2 = jnp.where(s2 == v2[a:a + 1, :], float(a), rank2)
            is_rank1 = lambda a: s1 == v1[a:a + 1, :]

        bcount = jnp.zeros((PEER_TOPK, t), F32)
        for a in range(half_k):
            cnt = jnp.sum(sel[a * PEER_TOPK:(a + 1) * PEER_TOPK, :], axis=0, keepdims=True)
            bcount = jnp.where(slot == a, cnt, bcount)
        tail = jnp.concatenate([jnp.zeros((half_k, t), F32), sel[half_k * PEER_TOPK:, :]], axis=0)
        bcount = jnp.where(slot >= half_k, tail, bcount)

        lim = jnp.zeros((PEER_N_KEYS, t), F32)
        for a in range(PEER_TOPK):
            lim = jnp.where(is_rank1(a), bcount[a:a + 1, :], lim)

        lim_ref[h] = lim
        e1_ref[h] = jnp.exp(s1 - v1[0:1, :])
        r2_ref[h] = rank2.astype(BF16)
        e2_ref[h] = (jnp.exp(s2 - v2[0:1, :]) / z).astype(BF16)
        return jnp.max(n_taken)

    def head(h, _):
        most = solve(h, exact=False)

        @pl.when(most > float(PEER_TOPK))
        def _():
            solve(h, exact=True)

        return 0

    lax.fori_loop(0, PEER_HEADS, head, 0)


def _topk(s1T, s2T, *, tt):
    L = s1T.shape[2]
    spec = pl.BlockSpec((PEER_HEADS, PEER_N_KEYS, tt), lambda i: (0, 0, i))
    shape = jax.ShapeDtypeStruct((PEER_HEADS, PEER_N_KEYS, L), F32)
    shape_b = jax.ShapeDtypeStruct((PEER_HEADS, PEER_N_KEYS, L), BF16)
    return pl.pallas_call(
        _topk_kernel,
        grid=(L // tt,),
        in_specs=[spec, spec],
        out_specs=[spec] * 4,
        out_shape=[shape, shape, shape_b, shape_b],
        compiler_params=pltpu.CompilerParams(
            dimension_semantics=("parallel",), vmem_limit_bytes=VMEM_LIMIT),
        name="topk",
    )(s1T, s2T)


def _peer_kernel(h2T_ref, u_ref, unext_ref, vT_ref, lim_ref, e1_ref, r2_ref, e2_ref, x1_ref, lnf_ref,
                 o_ref, acc_ref, act0_ref, *, et, sub):
    e = pl.program_id(1)
    n_sub = et // sub
    keys_per_sub = sub // PEER_N_KEYS

    def activation(rows_ref, s):
        return jnp.dot(rows_ref[s * sub:(s + 1) * sub, :], h2T_ref[...], preferred_element_type=F32)

    @pl.when(e == 0)
    def _():
        acc_ref[...] = jnp.zeros(acc_ref.shape, F32)
        act0_ref[...] = activation(u_ref, 0)

    def gate(s):
        parts = []
        for i in range(keys_per_sub):
            i1 = e * (et // PEER_N_KEYS) + s * keys_per_sub + i
            w = None
            for h in range(PEER_HEADS):
                lim_row = lim_ref[h, pl.ds(i1, 1), :].astype(BF16)
                e1_row = e1_ref[h, pl.ds(i1, 1), :].astype(BF16)
                term = jnp.where(r2_ref[h] < lim_row, e2_ref[h] * e1_row, jnp.zeros((), BF16))
                w = term if w is None else w + term
            parts.append(w)
        return jnp.concatenate(parts, axis=0)

    total = None
    a = act0_ref[...]
    for s in range(n_sub):
        a_next = activation(u_ref, s + 1) if s + 1 < n_sub else activation(unext_ref, 0)
        gelu = 0.5 * a * (1.0 + lax.erf(a * (2.0 ** -0.5)))
        wg = gelu.astype(BF16) * gate(s)
        part = jnp.dot(vT_ref[:, s * sub:(s + 1) * sub], wg, preferred_element_type=F32)
        total = part if total is None else total + part
        a = a_next
    act0_ref[...] = a
    acc_ref[...] += total

    @pl.when(e == pl.num_programs(1) - 1)
    def _():
        x2 = x1_ref[...] + acc_ref[...].T
        o_ref[...] = x2 * lax.rsqrt(jnp.mean(x2 * x2, axis=-1, keepdims=True) + EPS) * lnf_ref[...]


def _peer(h2T, u_b, vT_b, lim, e1, r2, e2, x1, lnf_w, *, tt, et, sub):
    L = x1.shape[0]
    tok = pl.BlockSpec((PEER_HEADS, PEER_N_KEYS, tt), lambda t, e: (0, 0, t), pipeline_mode=pl.Buffered(1))
    n_blocks = PEER_N_EXPERTS // et
    return pl.pallas_call(
        functools.partial(_peer_kernel, et=et, sub=sub),
        grid=(L // tt, n_blocks),
        in_specs=[
            pl.BlockSpec((D_MODEL, tt), lambda t, e: (0, t), pipeline_mode=pl.Buffered(1)),
            pl.BlockSpec((et, D_MODEL), lambda t, e: (e, 0)),
            pl.BlockSpec((sub, D_MODEL), lambda t, e: (jnp.minimum(e + 1, n_blocks - 1) * (et // sub), 0)),
            pl.BlockSpec((D_MODEL, et), lambda t, e: (0, e)),
            tok, tok, tok, tok,
            pl.BlockSpec((tt, D_MODEL), lambda t, e: (t, 0), pipeline_mode=pl.Buffered(1)),
            pl.BlockSpec((1, D_MODEL), lambda t, e: (0, 0)),
        ],
        out_specs=pl.BlockSpec((tt, D_MODEL), lambda t, e: (t, 0)),
        out_shape=jax.ShapeDtypeStruct((L, D_MODEL), F32),
        scratch_shapes=[
            pltpu.VMEM((D_MODEL, tt), F32),
            pltpu.VMEM((sub, tt), F32),
        ],
        compiler_params=pltpu.CompilerParams(
            dimension_semantics=("parallel", "arbitrary"), vmem_limit_bytes=VMEM_LIMIT),
        name="peer",
    )(h2T, u_b, u_b, vT_b, lim, e1, r2, e2, x1, lnf_w.reshape(1, D_MODEL))


def _tiles(L):
    return dict(
        inproj_tm=min(512, L), inproj_tn=2816,
        ssd_q=min(128, L), ssd_chunks=4 if L % 512 == 0 else 1,
        fox_tq=min(1024, L), fox_tk=min(512, L),
        mid_tm=min(256, L),
        topk_tt=min(512, L),
        peer_tt=min(512, L), peer_et=1024, peer_sub=512,
    )


def kernel(x, ln1_w, w_in, conv_w, conv_b, dt_bias, a_log, d_skip, ssd_norm_w, fox_f_bias,
           w_out, ln2_w, peer_wq, peer_k1, peer_k2, peer_u, peer_v, lnf_w):
    B, L, D = x.shape
    assert B == 1 and D == D_MODEL
    t = _tiles(L)
    x2d = x.reshape(L, D)

    c_z, c_xbc, c_dt, c_q = 0, SSD_INNER, 2560, 2576
    c_f = c_q + 3 * FOX_INNER
    w_main = jnp.concatenate([w_in[:, c_q:c_f], w_in[:, c_z:c_dt]], axis=1).astype(BF16)
    w_small = jnp.concatenate(
        [w_in[:, c_dt:c_q], w_in[:, c_f:], jnp.zeros((D, SMALL_DIM - SSD_HEADS - FOX_HEADS), F32)],
        axis=1).astype(BF16)

    main, small, smallT = _inproj(x2d, ln1_w, w_main, w_small, tm=t["inproj_tm"], tn=t["inproj_tn"])
    y_ssd, cum = _ssd(main, small, smallT, conv_w, conv_b, dt_bias, a_log, d_skip, ssd_norm_w,
                      fox_f_bias, q=t["ssd_q"], chunks=t["ssd_chunks"])
    y_fox = _fox(main, cum, tq=t["fox_tq"], tk=t["fox_tk"])
    x1, h2T, s1T, s2T = _mid(x2d, y_ssd, y_fox, w_out.astype(BF16), ln2_w, peer_wq.astype(BF16),
                             peer_k1.astype(BF16), peer_k2.astype(BF16), tm=t["mid_tm"])
    lim, e1, r2, e2 = _topk(s1T, s2T, tt=t["topk_tt"])
    out = _peer(h2T, peer_u.astype(BF16), peer_v.T.astype(BF16), lim, e1, r2, e2, x1, lnf_w,
                tt=t["peer_tt"], et=t["peer_et"], sub=t["peer_sub"])
    return out.reshape(B, L, D)
```
